```python
import functools
import jax
import jax.numpy as jnp
from jax import lax
import numpy as np

D_MODEL = 2048
BATCH = 32
SEQ = 256
DEPTH = 4
DEC_BATCH = 2
DEC_SEQ = 2048
PAST_LEN = 512

GRID_W = 64
MIX_W = D_MODEL
GROUP_W = MIX_W // 4
POOL_WINDOWS = (2, 4, 8, 16)
POOL_CH = GROUP_W // len(POOL_WINDOWS)
LRU_W = GROUP_W
LRU_BLOCKS = 8
LRU_BLOCK_W = LRU_W // LRU_BLOCKS
LRU_CONV = 4
LRU_C = 8.0
CM_W = GROUP_W
CM_CONV = 31
NA_HEADS = 8
NA_HEAD_DIM = GROUP_W // NA_HEADS
NA_KH = 8
NA_KW = 16
Q_BLOCK = 128
D_FF = ((8 * D_MODEL // 3 + 255) // 256) * 256
N_MOD = 6

P_POOL = 0
P_LRU_X = P_POOL + GROUP_W
P_LRU_G = P_LRU_X + LRU_W
P_CM = P_LRU_G + LRU_W
P_QKV = P_CM + 2 * CM_W
D_IN = P_QKV + 3 * GROUP_W

EPS = 1e-6
NEG_INF = -1e30

kernel_name = 'hybrid_pool_lru_conv_natten_prefix_step'


def rms_norm(x, g):
    xf = x.astype(jnp.float32)
    y = xf * lax.rsqrt(jnp.mean(xf * xf, axis=-1, keepdims=True) + EPS)
    return (y * g.astype(jnp.float32)).astype(x.dtype)


def layer_norm(x, g, b):
    xf = x.astype(jnp.float32)
    mu = jnp.mean(xf, axis=-1, keepdims=True)
    var = jnp.mean(jnp.square(xf - mu), axis=-1, keepdims=True)
    y = (xf - mu) * lax.rsqrt(var + EPS)
    return (y * g.astype(jnp.float32) + b.astype(jnp.float32)).astype(x.dtype)


def modulation(cond, w_ada, b_ada):
    m = jax.nn.silu(cond) @ w_ada + b_ada
    return jnp.split(m[:, None, :], N_MOD, axis=-1)


def modulate(h, shift, scale):
    return h * (1.0 + scale) + shift


def depthwise_conv(x, w, b, pad_left, pad_right):
    c = x.shape[-1]
    y = lax.conv_general_dilated(x, w[:, None, :].astype(x.dtype), window_strides=(1,),
                                 padding=[(pad_left, pad_right)],
                                 dimension_numbers=('NWC', 'WIO', 'NWC'),
                                 feature_group_count=c)
    return y + b


def pool_mixer(u, pool_w, pool_scale):
    b, t, _ = u.shape
    uf = u.astype(jnp.float32)
    cs = jnp.concatenate([jnp.zeros_like(uf[:, :1]), jnp.cumsum(uf, axis=1)], axis=1)
    pos = jnp.arange(t)
    outs = []
    for gi, w in enumerate(POOL_WINDOWS):
        lo = jnp.clip(pos - w // 2, 0, t)
        hi = jnp.clip(pos - w // 2 + w, 0, t)
        csg = cs[..., gi * POOL_CH:(gi + 1) * POOL_CH]
        s = jnp.take(csg, hi, axis=1) - jnp.take(csg, lo, axis=1)
        outs.append(s / (hi - lo).astype(jnp.float32)[None, :, None])
    pooled = (jnp.concatenate(outs, axis=-1) - uf).reshape(b, t, len(POOL_WINDOWS), POOL_CH)
    y = jnp.einsum('btgc,gcd->btgd', pooled, pool_w.astype(jnp.float32)).reshape(b, t, GROUP_W)
    return (y * pool_scale.astype(jnp.float32)).astype(u.dtype)


def lru_scan(a, bterm, h0):
    def combine(e1, e2):
        a1, b1 = e1
        a2, b2 = e2
        return a1 * a2, a2 * b1 + b2
    a_cum, b_cum = lax.associative_scan(combine, (a, bterm), axis=1)
    return b_cum + a_cum * h0[:, None, :]


def rg_lru_mixer(xb, gb, conv_w, conv_b, gate_w, gate_b, lam, h0):
    b, t, _ = xb.shape
    x = depthwise_conv(xb, conv_w, conv_b, LRU_CONV // 2, LRU_CONV - 1 - LRU_CONV // 2)
    xf = x.astype(jnp.float32)
    xh = xf.reshape(b, t, LRU_BLOCKS, LRU_BLOCK_W)
    gates = jnp.einsum('btkc,dgkce->dgbtke', xh, gate_w.astype(jnp.float32)).reshape(2, 2, b, t, LRU_W)
    gates = gates + gate_b.astype(jnp.float32)[:, :, None, None, :]
    r = jax.nn.sigmoid(gates[:, 0])
    i = jax.nn.sigmoid(gates[:, 1])
    log_a = -LRU_C * r * jax.nn.softplus(-lam.astype(jnp.float32))[:, None, None, :]
    a = jnp.exp(log_a)
    bterm = jnp.sqrt(jnp.maximum(-jnp.expm1(2.0 * log_a), 0.0)) * (i * xf[None])
    h0f = h0.astype(jnp.float32)
    h_f = lru_scan(a[0], bterm[0], h0f[:, 0])
    h_b = jnp.flip(lru_scan(jnp.flip(a[1], axis=1), jnp.flip(bterm[1], axis=1), h0f[:, 1]), axis=1)
    y = (h_f + h_b) * jax.nn.gelu(gb.astype(jnp.float32))
    return y.astype(xb.dtype), h_f, h_b


def conformer_conv(u2, dw_w, dw_b, ln_g, ln_b, pw_w, pw_b):
    a, g = jnp.split(u2, 2, axis=-1)
    u = a * jax.nn.sigmoid(g)
    u = depthwise_conv(u, dw_w, dw_b, CM_CONV // 2, CM_CONV // 2)
    u = jax.nn.silu(layer_norm(u, ln_g, ln_b))
    return u @ pw_w + pw_b


def context_attention(q, k, v):
    b, t, nh, hd = q.shape
    scale = hd ** -0.5
    qb = jnp.moveaxis(q.reshape(b, t // Q_BLOCK, Q_BLOCK, nh, hd), 1, 0)

    def one_block(q_blk):
        s = jnp.einsum('bqhd,bkhd->bhqk', q_blk, k).astype(jnp.float32) * scale
        p = jax.nn.softmax(s, axis=-1).astype(v.dtype)
        return jnp.einsum('bhqk,bkhd->bqhd', p, v)

    o = lax.map(one_block, qb)
    return jnp.moveaxis(o, 0, 1).reshape(b, t, nh * hd)


def latent_neighbourhood_attention(q, k, v, ctx_k, ctx_v, rpb):
    b, t, nh, hd = q.shape
    rows = t // GRID_W
    kh = min(NA_KH, rows)
    scale = hd ** -0.5
    kg = k.reshape(b, rows, GRID_W, nh, hd)
    vg = v.reshape(b, rows, GRID_W, nh, hd)
    qg = jnp.moveaxis(q.reshape(b, rows, GRID_W, nh, hd), 1, 0)
    col = jnp.arange(GRID_W)
    col_start = jnp.clip(col - NA_KW // 2, 0, GRID_W - NA_KW)
    col_in = (col[None, :] >= col_start[:, None]) & (col[None, :] < col_start[:, None] + NA_KW)
    col_idx = jnp.clip(col[None, :] - col[:, None] + NA_KW - 1, 0, 2 * NA_KW - 2)
    n_nb = kh * GRID_W

    def one_row(args):
        r, q_r = args
        r0 = jnp.clip(r - kh // 2, 0, rows - kh)
        k_r = lax.dynamic_slice_in_dim(kg, r0, kh, axis=1)
        v_r = lax.dynamic_slice_in_dim(vg, r0, kh, axis=1)
        row_idx = r0 + jnp.arange(kh) - r + NA_KH - 1
        bias = rpb[:, row_idx][:, :, col_idx].transpose(0, 2, 1, 3).astype(jnp.float32)
        s_nb = jnp.einsum('bqhd,bjkhd->bhqjk', q_r, k_r).astype(jnp.float32) * scale + bias[None]
        s_nb = jnp.where(col_in[None, None, :, None, :], s_nb, NEG_INF)
        s_ctx = jnp.einsum('bqhd,blhd->bhql', q_r, ctx_k).astype(jnp.float32) * scale
        s = jnp.concatenate([s_nb.reshape(b, nh, GRID_W, n_nb), s_ctx], axis=-1)
        p = jax.nn.softmax(s, axis=-1).astype(v.dtype)
        p_nb = p[..., :n_nb].reshape(b, nh, GRID_W, kh, GRID_W)
        return (jnp.einsum('bhqjk,bjkhd->bqhd', p_nb, v_r)
                + jnp.einsum('bhql,blhd->bqhd', p[..., n_nb:], ctx_v))

    o = lax.map(one_row, (jnp.arange(rows), qg))
    return jnp.moveaxis(o, 0, 1).reshape(b, t, nh * hd)


def mixer_heads(h, w_in, pool_w, pool_scale, lru_conv_w, lru_conv_b, lru_gate_w, lru_gate_b,
                lru_lambda, cm_dw_w, cm_dw_b, cm_ln_g, cm_ln_b, cm_pw_w, cm_pw_b, w_out, h0, attend):
    b, t, _ = h.shape
    u = h @ w_in
    y_pool = pool_mixer(u[..., P_POOL:P_LRU_X], pool_w, pool_scale)
    y_lru, h_f, h_b = rg_lru_mixer(u[..., P_LRU_X:P_LRU_G], u[..., P_LRU_G:P_CM], lru_conv_w, lru_conv_b,
                                   lru_gate_w, lru_gate_b, lru_lambda, h0)
    y_cm = conformer_conv(u[..., P_CM:P_QKV], cm_dw_w, cm_dw_b, cm_ln_g, cm_ln_b, cm_pw_w, cm_pw_b)
    qkv = u[..., P_QKV:D_IN].reshape(b, t, 3, NA_HEADS, NA_HEAD_DIM)
    q, k, v = qkv[:, :, 0], qkv[:, :, 1], qkv[:, :, 2]
    y_att = attend(q, k, v)
    out = jnp.concatenate([y_pool, y_lru, y_cm, y_att], axis=-1) @ w_out
    return out, k, v, h_f, h_b


def ffn_sublayer(x, shift, scale, gate, g_pre, g_post, w1, w3, w2):
    h = modulate(rms_norm(x, g_pre), shift, scale)
    y = (jax.nn.silu(h @ w1) * (h @ w3)) @ w2
    return x + gate * rms_norm(y, g_post)


def setup_inputs(seed: int = 0) -> dict:
    key = jax.random.key(seed)
    ks = jax.random.split(key, 40)
    f32 = jnp.float32
    nrm = lambda k, shape, s: jax.random.normal(k, shape, f32) * s
    D = D_MODEL
    return {
        'x_prompt': nrm(ks[0], (BATCH, SEQ, D), 1.0),
        'x_sample': nrm(ks[1], (DEC_BATCH, DEC_SEQ, D), 1.0),
        'cache_k': nrm(ks[2], (DEC_BATCH, DEPTH, PAST_LEN, NA_HEADS, NA_HEAD_DIM), 1.0),
        'cache_v': nrm(ks[3], (DEC_BATCH, DEPTH, PAST_LEN, NA_HEADS, NA_HEAD_DIM), 1.0),
        'state_lru': nrm(ks[4], (DEC_BATCH, DEPTH, 2, LRU_W), 0.5),
        'c': nrm(ks[5], (DEC_BATCH, D), 1.0),
        'c_ctx': nrm(ks[6], (D,), 1.0),
        'w_ada': nrm(ks[7], (DEPTH, D, N_MOD * D), 0.5 * D ** -0.5),
        'b_ada': nrm(ks[8], (DEPTH, N_MOD * D), 0.02),
        'g_pre_mix': 1.0 + nrm(ks[9], (DEPTH, D), 0.05),
        'g_post_mix': 1.0 + nrm(ks[10], (DEPTH, D), 0.05),
        'g_pre_ffn': 1.0 + nrm(ks[11], (DEPTH, D), 0.05),
        'g_post_ffn': 1.0 + nrm(ks[12], (DEPTH, D), 0.05),
        'w_in': nrm(ks[13], (DEPTH, D, D_IN), D ** -0.5),
        'pool_w': nrm(ks[14], (DEPTH, len(POOL_WINDOWS), POOL_CH, POOL_CH), POOL_CH ** -0.5),
        'pool_scale': 1.0 + nrm(ks[15], (DEPTH, GROUP_W), 0.05),
        'lru_conv_w': nrm(ks[16], (DEPTH, LRU_CONV, LRU_W), 0.5),
        'lru_conv_b': nrm(ks[17], (DEPTH, LRU_W), 0.02),
        'lru_gate_w': nrm(ks[18], (DEPTH, 2, 2, LRU_BLOCKS, LRU_BLOCK_W, LRU_BLOCK_W), LRU_BLOCK_W ** -0.5),
        'lru_gate_b': nrm(ks[19], (DEPTH, 2, 2, LRU_W), 0.02),
        'lru_lambda': jax.random.uniform(ks[20], (DEPTH, 2, LRU_W), f32, 4.0, 9.0),
        'cm_dw_w': nrm(ks[21], (DEPTH, CM_CONV, CM_W), CM_CONV ** -0.5),
        'cm_dw_b': nrm(ks[22], (DEPTH, CM_W), 0.02),
        'cm_ln_g': 1.0 + nrm(ks[23], (DEPTH, CM_W), 0.05),
        'cm_ln_b': nrm(ks[24], (DEPTH, CM_W), 0.02),
        'cm_pw_w': nrm(ks[25], (DEPTH, CM_W, CM_W), CM_W ** -0.5),
        'cm_pw_b': nrm(ks[26], (DEPTH, CM_W), 0.02),
        'na_rpb': nrm(ks[27], (DEPTH, NA_HEADS, 2 * NA_KH - 1, 2 * NA_KW - 1), 0.1),
        'w_out': nrm(ks[28], (DEPTH, MIX_W, D), MIX_W ** -0.5),
        'ffn_w1': nrm(ks[29], (DEPTH, D, D_FF), D ** -0.5),
        'ffn_w3': nrm(ks[30], (DEPTH, D, D_FF), D ** -0.5),
        'ffn_w2': nrm(ks[31], (DEPTH, D_FF, D), D_FF ** -0.5),
    }


def reference(x_prompt, x_sample, cache_k, cache_v, state_lru, c, c_ctx, w_ada, b_ada,
              g_pre_mix, g_post_mix, g_pre_ffn, g_post_ffn, w_in, pool_w, pool_scale,
              lru_conv_w, lru_conv_b, lru_gate_w, lru_gate_b, lru_lambda, cm_dw_w, cm_dw_b,
              cm_ln_g, cm_ln_b, cm_pw_w, cm_pw_b, na_rpb, w_out, ffn_w1, ffn_w3, ffn_w2):
    xp = x_prompt
    xs = x_sample
    ks_out, vs_out, st_out = [], [], []
    for l in range(DEPTH):
        mix_w = (w_in[l], pool_w[l], pool_scale[l], lru_conv_w[l], lru_conv_b[l], lru_gate_w[l],
                 lru_gate_b[l], lru_lambda[l], cm_dw_w[l], cm_dw_b[l], cm_ln_g[l], cm_ln_b[l],
                 cm_pw_w[l], cm_pw_b[l], w_out[l])
        ffn_w = (g_pre_ffn[l], g_post_ffn[l], ffn_w1[l], ffn_w3[l], ffn_w2[l])

        sh1, sc1, gt1, sh2, sc2, gt2 = modulation(c_ctx[None, :], w_ada[l], b_ada[l])
        hp = modulate(rms_norm(xp, g_pre_mix[l]), sh1, sc1)
        h0 = jnp.zeros((xp.shape[0], 2, LRU_W), jnp.float32)
        mix, k_c, v_c, hf_c, hb_c = mixer_heads(hp, *mix_w, h0=h0, attend=context_attention)
        xp = xp + gt1 * rms_norm(mix, g_post_mix[l])
        xp = ffn_sublayer(xp, sh2, sc2, gt2, *ffn_w)
        ks_out.append(k_c)
        vs_out.append(v_c)
        st_out.append(jnp.stack([hf_c[:, -1], hb_c[:, 0]], axis=1).astype(xp.dtype))

        sh1, sc1, gt1, sh2, sc2, gt2 = modulation(c, w_ada[l], b_ada[l])
        hs = modulate(rms_norm(xs, g_pre_mix[l]), sh1, sc1)
        attend_lat = functools.partial(latent_neighbourhood_attention, ctx_k=cache_k[:, l],
                                       ctx_v=cache_v[:, l], rpb=na_rpb[l])
        mix, _, _, _, _ = mixer_heads(hs, *mix_w, h0=state_lru[:, l], attend=attend_lat)
        xs = xs + gt1 * rms_norm(mix, g_post_mix[l])
        xs = ffn_sublayer(xs, sh2, sc2, gt2, *ffn_w)

    new_cache_k = jnp.stack(ks_out, axis=1)
    new_cache_v = jnp.stack(vs_out, axis=1)
    new_state_lru = jnp.stack(st_out, axis=1)
    return (xp, xs, new_cache_k, new_cache_v, new_state_lru)
```

```python
import functools

import jax
import jax.numpy as jnp
from jax import lax
from jax.experimental import pallas as pl
from jax.experimental.pallas import tpu as pltpu

F32 = jnp.float32
BF16 = jnp.bfloat16

EPS = 1e-6
NEG_INF = -1e30
N_MOD = 6
POOL_WINDOWS = (2, 4, 8, 16)
LRU_CONV = 4
LRU_C = 8.0
CM_CONV = 31
NA_HEADS = 8
NA_KH = 8
NA_KW = 16
GRID_W = 64

LANES = 128
SUBLANES = 8
VMEM_LIMIT_BYTES = 56 * 1024 * 1024
SEQ_CHUNK = 256
HALO = 16


def _params(*sem):
    return pltpu.CompilerParams(dimension_semantics=sem, vmem_limit_bytes=VMEM_LIMIT_BYTES)


def _sigmoid(x):
    return 1.0 / (1.0 + jnp.exp(-x))


def _adaln_kernel(cond_ref, w_ref, b_ref, o_ref):
    c = cond_ref[...]
    s = (c * _sigmoid(c)).astype(BF16)
    o_ref[...] = jnp.dot(s, w_ref[...].astype(BF16), preferred_element_type=F32) + b_ref[...]


def adaln_all(cond, w_ada, b_ada, tn=1024):
    depth, d, n = w_ada.shape
    return pl.pallas_call(
        _adaln_kernel,
        out_shape=jax.ShapeDtypeStruct((depth, 8, n), F32),
        grid=(depth, n // tn),
        in_specs=[
            pl.BlockSpec((8, d), lambda l, j: (0, 0)),
            pl.BlockSpec((None, d, tn), lambda l, j: (l, 0, j)),
            pl.BlockSpec((None, 1, tn), lambda l, j: (l, 0, j)),
        ],
        out_specs=pl.BlockSpec((None, 8, tn), lambda l, j: (l, 0, j)),
        compiler_params=_params("parallel", "parallel"),
        name="adaln",
    )(cond, w_ada, b_ada.reshape(depth, 1, n))


def _rms(x, g):
    ms = jnp.mean(x * x, axis=-1, keepdims=True)
    return x * lax.rsqrt(ms + EPS) * g


def _in_proj_kernel(x_ref, mod_ref, g_ref, w_ref, o_ref, h_ref, *, d):
    @pl.when(pl.program_id(1) == 0)
    def _():
        y = _rms(x_ref[...], g_ref[...])
        shift = mod_ref[:, 0:d]
        scale = mod_ref[:, d:2 * d]
        h_ref[...] = (y * (1.0 + scale) + shift).astype(BF16)

    o_ref[...] = jnp.dot(h_ref[...], w_ref[...], preferred_element_type=F32)


def _cond_index(i, tm, seq_len, cond_base, cond_stride):
    return cond_base + ((i * tm) // seq_len) * cond_stride


def in_proj(x, mods, g_pre, w_in, layer, seq_len, cond_base, cond_stride, tm=512, tn=1024):
    m, d = x.shape
    n = w_in.shape[-1]
    cidx = functools.partial(_cond_index, tm=tm, seq_len=seq_len, cond_base=cond_base, cond_stride=cond_stride)
    return pl.pallas_call(
        functools.partial(_in_proj_kernel, d=d),
        out_shape=jax.ShapeDtypeStruct((m, n), F32),
        grid=(m // tm, n // tn),
        in_specs=[
            pl.BlockSpec((tm, d), lambda i, j: (i, 0)),
            pl.BlockSpec((None, None, 1, N_MOD * d), lambda i, j: (layer, cidx(i), 0, 0)),
            pl.BlockSpec((None, 1, d), lambda i, j: (layer, 0, 0)),
            pl.BlockSpec((None, d, tn), lambda i, j: (layer, 0, j)),
        ],
        out_specs=pl.BlockSpec((tm, tn), lambda i, j: (i, j)),
        scratch_shapes=[pltpu.VMEM((tm, d), BF16)],
        compiler_params=_params("parallel", "arbitrary"),
        name="in_proj",
    )(x, mods, g_pre, w_in)


def _halo_specs(width, col_block, seq_len):
    chunks = seq_len // SEQ_CHUNK
    per = SEQ_CHUNK // HALO

    def prev_map(s, c):
        return (jnp.maximum((s * chunks + c) * per - 1, 0), col_block)

    def next_map(s, c, nblk):
        return (jnp.minimum((s * chunks + c + 1) * per, nblk - 1), col_block)

    return chunks, per, prev_map, next_map


def _fill_padded(pad_ref, prev_ref, cur, next_ref, chunks):
    c = pl.program_id(1)
    prev = jnp.where(c == 0, 0.0, prev_ref[...])
    nxt = jnp.where(c == chunks - 1, 0.0, next_ref[...])
    pad_ref[0:HALO, :] = prev
    pad_ref[HALO:HALO + SEQ_CHUNK, :] = cur
    pad_ref[HALO + SEQ_CHUNK:2 * HALO + SEQ_CHUNK, :] = nxt


def _pool_kernel(prev_ref, cur_ref, next_ref, w_ref, sc_ref, o_ref, pad_ref, *, chunks, seq_len):
    _fill_padded(pad_ref, prev_ref, cur_ref[...], next_ref, chunks)
    ch = LANES
    t = pl.program_id(1) * SEQ_CHUNK + lax.broadcasted_iota(jnp.int32, (SEQ_CHUNK, ch), 0)
    for gi, w in enumerate(POOL_WINDOWS):
        cols = slice(gi * ch, (gi + 1) * ch)
        acc = pad_ref[HALO - w // 2:HALO - w // 2 + SEQ_CHUNK, cols]
        for off in range(-(w // 2) + 1, w - w // 2):
            acc = acc + pad_ref[HALO + off:HALO + off + SEQ_CHUNK, cols]
        lo = jnp.clip(t - w // 2, 0, seq_len)
        hi = jnp.clip(t - w // 2 + w, 0, seq_len)
        pooled = acc / (hi - lo).astype(F32) - cur_ref[:, cols]
        y = jnp.dot(pooled.astype(BF16), w_ref[gi], preferred_element_type=F32) * sc_ref[:, cols]
        o_ref[:, cols] = y.astype(o_ref.dtype)


def pool_mixer(u, pool_w, pool_scale, layer, seq_len, col_block):
    m = u.shape[0]
    width = len(POOL_WINDOWS) * LANES
    n_seq = m // seq_len
    chunks, per, prev_map, next_map = _halo_specs(width, col_block, seq_len)
    nblk = m // HALO
    return pl.pallas_call(
        functools.partial(_pool_kernel, chunks=chunks, seq_len=seq_len),
        out_shape=jax.ShapeDtypeStruct((m, width), BF16),
        grid=(n_seq, chunks),
        in_specs=[
            pl.BlockSpec((HALO, width), prev_map),
            pl.BlockSpec((SEQ_CHUNK, width), lambda s, c: (s * chunks + c, col_block)),
            pl.BlockSpec((HALO, width), functools.partial(next_map, nblk=nblk)),
            pl.BlockSpec((None, len(POOL_WINDOWS), LANES, LANES), lambda s, c: (layer, 0, 0, 0)),
            pl.BlockSpec((None, 1, width), lambda s, c: (layer, 0, 0)),
        ],
        out_specs=pl.BlockSpec((SEQ_CHUNK, width), lambda s, c: (s * chunks + c, 0)),
        scratch_shapes=[pltpu.VMEM((SEQ_CHUNK + 2 * HALO, width), F32)],
        compiler_params=_params("parallel", "parallel"),
        name="pool",
    )(u, u, u, pool_w, pool_scale)


def _conformer_kernel(ap_ref, a_ref, an_ref, gp_ref, g_ref, gn_ref, dw_ref, dwb_ref, lng_ref, lnb_ref,
                      pw_ref, pwb_ref, o_ref, pad_ref, *, chunks):
    c = pl.program_id(1)
    glu = lambda a, g: a * _sigmoid(g)
    pad_ref[0:HALO, :] = jnp.where(c == 0, 0.0, glu(ap_ref[...], gp_ref[...]))
    pad_ref[HALO:HALO + SEQ_CHUNK, :] = glu(a_ref[...], g_ref[...])
    pad_ref[HALO + SEQ_CHUNK:2 * HALO + SEQ_CHUNK, :] = jnp.where(c == chunks - 1, 0.0, glu(an_ref[...], gn_ref[...]))
    half = CM_CONV // 2
    acc = dwb_ref[...] + dw_ref[0:1, :] * pad_ref[HALO - half:HALO - half + SEQ_CHUNK, :]
    for k in range(1, CM_CONV):
        acc = acc + dw_ref[k:k + 1, :] * pad_ref[HALO - half + k:HALO - half + k + SEQ_CHUNK, :]
    mu = jnp.mean(acc, axis=-1, keepdims=True)
    cen = acc - mu
    var = jnp.mean(cen * cen, axis=-1, keepdims=True)
    y = cen * lax.rsqrt(var + EPS) * lng_ref[...] + lnb_ref[...]
    z = (y * _sigmoid(y)).astype(BF16)
    o = jnp.dot(z, pw_ref[...], preferred_element_type=F32) + pwb_ref[...]
    o_ref[...] = o.astype(o_ref.dtype)


def conformer(u, dw_w, dw_b, ln_g, ln_b, pw_w, pw_b, layer, seq_len, col_block_a):
    m = u.shape[0]
    width = dw_w.shape[-1]
    n_seq = m // seq_len
    nblk = m // HALO
    chunks, per, prev_a, next_a = _halo_specs(width, col_block_a, seq_len)
    _, _, prev_g, next_g = _halo_specs(width, col_block_a + 1, seq_len)
    vec = lambda: pl.BlockSpec((None, 1, width), lambda s, c: (layer, 0, 0))
    return pl.pallas_call(
        functools.partial(_conformer_kernel, chunks=chunks),
        out_shape=jax.ShapeDtypeStruct((m, width), BF16),
        grid=(n_seq, chunks),
        in_specs=[
            pl.BlockSpec((HALO, width), prev_a),
            pl.BlockSpec((SEQ_CHUNK, width), lambda s, c: (s * chunks + c, col_block_a)),
            pl.BlockSpec((HALO, width), functools.partial(next_a, nblk=nblk)),
            pl.BlockSpec((HALO, width), prev_g),
            pl.BlockSpec((SEQ_CHUNK, width), lambda s, c: (s * chunks + c, col_block_a + 1)),
            pl.BlockSpec((HALO, width), functools.partial(next_g, nblk=nblk)),
            pl.BlockSpec((None, CM_CONV, width), lambda s, c: (layer, 0, 0)),
            vec(), vec(), vec(),
            pl.BlockSpec((None, width, width), lambda s, c: (layer, 0, 0)),
            vec(),
        ],
        out_specs=pl.BlockSpec((SEQ_CHUNK, width), lambda s, c: (s * chunks + c, 0)),
        scratch_shapes=[pltpu.VMEM((SEQ_CHUNK + 2 * HALO, width), F32)],
        compiler_params=_params("parallel", "parallel"),
        name="conformer",
    )(u, u, u, u, u, u, dw_w, dw_b, ln_g, ln_b, pw_w, pw_b)


def _softplus(z):
    return jnp.maximum(z, 0.0) + jnp.log1p(jnp.exp(-jnp.abs(z)))


def _gelu_tanh(x):
    return 0.5 * x * (1.0 + jnp.tanh(0.7978845608028654 * (x + 0.044715 * (x * x * x))))


def _row_select(rows):
    sub = lax.broadcasted_iota(jnp.int32, (SUBLANES, LANES), 0)
    out = jnp.zeros((SUBLANES, LANES), F32)
    for s, r in enumerate(rows):
        out = jnp.where(sub == s, r, out)
    return out


def _lru_kernel(x_ref, g_ref, cw_ref, cb_ref, gw_ref, gb_ref, lam_ref, h0_ref, y_ref, st_ref,
                xpad, a_f, b_f, a_b, b_b, *, seq_len):
    t_len = seq_len
    seg = t_len // SUBLANES
    pitch = seg + 4
    zeros = jnp.zeros((SUBLANES, LANES), F32)
    xpad[0:SUBLANES, :] = zeros
    xpad[SUBLANES + t_len:2 * SUBLANES + t_len, :] = zeros
    xpad[SUBLANES:SUBLANES + t_len, :] = x_ref[...]
    left = LRU_CONV // 2
    x = cb_ref[...] + cw_ref[0:1, :] * xpad[SUBLANES - left:SUBLANES - left + t_len, :]
    for k in range(1, LRU_CONV):
        x = x + cw_ref[k:k + 1, :] * xpad[SUBLANES - left + k:SUBLANES - left + k + t_len, :]
    xb = x.astype(BF16)
    sp = _softplus(-lam_ref[...])
    for d, (a_ref, b_ref) in enumerate(((a_f, b_f), (a_b, b_b))):
        r = _sigmoid(jnp.dot(xb, gw_ref[2 * d], preferred_element_type=F32) + gb_ref[2 * d:2 * d + 1, :])
        i = _sigmoid(jnp.dot(xb, gw_ref[2 * d + 1], preferred_element_type=F32) + gb_ref[2 * d + 1:2 * d + 2, :])
        log_a = (-LRU_C) * r * sp[d:d + 1, :]
        a = jnp.exp(log_a)
        bt = jnp.sqrt(jnp.maximum(-jnp.tanh(log_a) * (a * a + 1.0), 0.0)) * (i * x)
        for s in range(SUBLANES):
            a_ref[s * pitch:s * pitch + seg, :] = a[s * seg:(s + 1) * seg, :]
            b_ref[s * pitch:s * pitch + seg, :] = bt[s * seg:(s + 1) * seg, :]

    def rows(ref, j):
        return ref[pl.ds(j, SUBLANES, stride=pitch), :]

    def totals(j, carry):
        hf, pf, hb, pb = carry
        af = rows(a_f, j)
        ab = rows(a_b, seg - 1 - j)
        return (af * hf + rows(b_f, j), af * pf, ab * hb + rows(b_b, seg - 1 - j), ab * pb)

    ones = jnp.ones((SUBLANES, LANES), F32)
    hf, pf, hb, pb = lax.fori_loop(0, seg, totals, (zeros, ones, zeros, ones))

    cf = [h0_ref[0:1, :]]
    for s in range(SUBLANES):
        cf.append(hf[s:s + 1, :] + pf[s:s + 1, :] * cf[-1])
    cb = [h0_ref[1:2, :]]
    for s in reversed(range(SUBLANES)):
        cb.append(hb[s:s + 1, :] + pb[s:s + 1, :] * cb[-1])
    st_ref[0:1, :] = cf[SUBLANES]
    st_ref[1:2, :] = cb[SUBLANES]
    cin_f = _row_select(cf[:SUBLANES])
    cin_b = _row_select(list(reversed(cb[:SUBLANES])))

    def states(j, carry):
        hf, hb = carry
        jb = seg - 1 - j
        hf = rows(a_f, j) * hf + rows(b_f, j)
        hb = rows(a_b, jb) * hb + rows(b_b, jb)
        b_f[pl.ds(j, SUBLANES, stride=pitch), :] = hf
        b_b[pl.ds(jb, SUBLANES, stride=pitch), :] = hb
        return hf, hb

    lax.fori_loop(0, seg, states, (cin_f, cin_b))

    for s in range(SUBLANES):
        h = b_f[s * pitch:s * pitch + seg, :] + b_b[s * pitch:s * pitch + seg, :]
        y_ref[s * seg:(s + 1) * seg, :] = (h * _gelu_tanh(g_ref[s * seg:(s + 1) * seg, :])).astype(y_ref.dtype)


def lru_mixer(u, conv_w, conv_b, gate_w, gate_b, lam, h0, layer, seq_len, col_block_x, col_block_g):
    m = u.shape[0]
    width = conv_w.shape[-1]
    nb = width // LANES
    n_seq = m // seq_len
    pitch_rows = SUBLANES * (seq_len // SUBLANES + 4)
    scan_buf = pltpu.VMEM((pitch_rows, LANES), F32)
    return pl.pallas_call(
        functools.partial(_lru_kernel, seq_len=seq_len),
        out_shape=(jax.ShapeDtypeStruct((m, width), BF16), jax.ShapeDtypeStruct((n_seq, 2, width), F32)),
        grid=(n_seq, nb),
        in_specs=[
            pl.BlockSpec((seq_len, LANES), lambda s, c: (s, col_block_x * nb + c)),
            pl.BlockSpec((seq_len, LANES), lambda s, c: (s, col_block_g * nb + c)),
            pl.BlockSpec((None, LRU_CONV, LANES), lambda s, c: (layer, 0, c)),
            pl.BlockSpec((None, 1, LANES), lambda s, c: (layer, 0, c)),
            pl.BlockSpec((None, 4, None, LANES, LANES), lambda s, c: (layer, 0, c, 0, 0)),
            pl.BlockSpec((None, 4, LANES), lambda s, c: (layer, 0, c)),
            pl.BlockSpec((None, 2, LANES), lambda s, c: (layer, 0, c)),
            pl.BlockSpec((None, 2, LANES), lambda s, c: (s, 0, c)),
        ],
        out_specs=(
            pl.BlockSpec((seq_len, LANES), lambda s, c: (s, c)),
            pl.BlockSpec((None, 2, LANES), lambda s, c: (s, 0, c)),
        ),
        scratch_shapes=[pltpu.VMEM((seq_len + 2 * SUBLANES, LANES), F32), scan_buf, scan_buf, scan_buf, scan_buf],
        compiler_params=_params("parallel", "parallel"),
        name="lru",
    )(u, u, conv_w, conv_b, gate_w, gate_b, lam, h0)


def _head_masks():
    lane = lax.broadcasted_iota(jnp.int32, (1, LANES), 1)
    return lane < LANES // 2, lane >= LANES // 2


def _nt_dot(a, b):
    return lax.dot_general(a, b, (((1,), (1,)), ((), ())), preferred_element_type=F32)


def _ctx_attn_kernel(q_ref, k_ref, v_ref, o_ref, *, head_dim):
    scale = head_dim ** -0.5
    masks = _head_masks()
    for p in range(q_ref.shape[-1] // LANES):
        cols = slice(p * LANES, (p + 1) * LANES)
        q = q_ref[:, cols]
        k = k_ref[:, cols].astype(BF16)
        v = v_ref[:, cols]
        out = None
        for msk in masks:
            s = _nt_dot(jnp.where(msk, q, 0.0).astype(BF16), k) * scale
            e = jnp.exp(s - jnp.max(s, axis=-1, keepdims=True))
            den = jnp.sum(e, axis=-1, keepdims=True)
            o = jnp.dot(e.astype(BF16), jnp.where(msk, v, 0.0).astype(BF16), preferred_element_type=F32) / den
            out = o if out is None else out + o
        o_ref[:, cols] = out.astype(o_ref.dtype)


def context_attention(u, seq_len, col_block_q, width, head_dim):
    m = u.shape[0]
    n_seq = m // seq_len
    spec = lambda cb: pl.BlockSpec((seq_len, width), lambda s: (s, cb))
    return pl.pallas_call(
        functools.partial(_ctx_attn_kernel, head_dim=head_dim),
        out_shape=jax.ShapeDtypeStruct((m, width), BF16),
        grid=(n_seq,),
        in_specs=[spec(col_block_q), spec(col_block_q + 1), spec(col_block_q + 2)],
        out_specs=pl.BlockSpec((seq_len, width), lambda s: (s, 0)),
        compiler_params=_params("parallel"),
        name="ctx_attn",
    )(u, u, u)


def _rpb_tile_kernel(rpb_ref, o_ref):
    qc = lax.broadcasted_iota(jnp.int32, (GRID_W, GRID_W), 0)
    kc = lax.broadcasted_iota(jnp.int32, (GRID_W, GRID_W), 1)
    start = jnp.clip(qc - NA_KW // 2, 0, GRID_W - NA_KW)
    inside = (kc >= start) & (kc < start + NA_KW)
    idx = jnp.clip(kc - qc + NA_KW - 1, 0, 2 * NA_KW - 2)
    nd = 2 * NA_KW - 1
    row = (pl.program_id(0) * pl.num_programs(1) + pl.program_id(1)) * pl.num_programs(2) + pl.program_id(2)
    acc = jnp.zeros((GRID_W, GRID_W), F32)
    for d in range(nd):
        acc = jnp.where(idx == d, rpb_ref[row * nd + d], acc)
    o_ref[...] = jnp.where(inside, acc, NEG_INF)


def rpb_tiles(rpb):
    depth, heads, nr, _ = rpb.shape
    return pl.pallas_call(
        _rpb_tile_kernel,
        out_shape=jax.ShapeDtypeStruct((depth, heads, nr, GRID_W, GRID_W), F32),
        grid=(depth, heads, nr),
        in_specs=[pl.BlockSpec(memory_space=pltpu.SMEM)],
        out_specs=pl.BlockSpec((None, None, None, GRID_W, GRID_W), lambda l, h, r: (l, h, r, 0, 0)),
        compiler_params=_params("parallel", "parallel", "parallel"),
        name="rpb_tiles",
    )(rpb.reshape(-1))


def _nbr_attn_kernel(q_ref, k_ref, v_ref, ck_ref, cv_ref, bias_ref, o_ref, *, head_dim, rows, kh):
    scale = head_dim ** -0.5
    r = pl.program_id(1)
    r0 = jnp.clip(r - kh // 2, 0, rows - kh)
    win = pl.ds(pl.multiple_of(r0 * GRID_W, GRID_W), kh * GRID_W)
    masks = _head_masks()
    for p in range(q_ref.shape[-1] // LANES):
        cols = slice(p * LANES, (p + 1) * LANES)
        q = q_ref[:, cols]
        k = k_ref[win, cols].astype(BF16)
        v = v_ref[win, cols]
        ck = ck_ref[:, cols].astype(BF16)
        cv = cv_ref[:, cols]
        out = None
        for hh, msk in enumerate(masks):
            qm = jnp.where(msk, q, 0.0).astype(BF16)
            s_nb = _nt_dot(qm, k) * scale + bias_ref[2 * p + hh]
            s_cx = _nt_dot(qm, ck) * scale
            mx = jnp.maximum(jnp.max(s_nb, axis=-1, keepdims=True), jnp.max(s_cx, axis=-1, keepdims=True))
            e_nb = jnp.exp(s_nb - mx)
            e_cx = jnp.exp(s_cx - mx)
            den = jnp.sum(e_nb, axis=-1, keepdims=True) + jnp.sum(e_cx, axis=-1, keepdims=True)
            o = jnp.dot(e_nb.astype(BF16), jnp.where(msk, v, 0.0).astype(BF16), preferred_element_type=F32)
            o = o + jnp.dot(e_cx.astype(BF16), jnp.where(msk, cv, 0.0).astype(BF16), preferred_element_type=F32)
            o = o / den
            out = o if out is None else out + o
        o_ref[:, cols] = out.astype(o_ref.dtype)


def neighbourhood_attention(u, ctx_k, ctx_v, bias, layer, seq_len, col_block_q, width, head_dim):
    m = u.shape[0]
    n_seq = m // seq_len
    rows = seq_len // GRID_W
    kh = min(NA_KH, rows)
    past = ctx_k.shape[2]
    heads = width // head_dim

    def bias_map(b, r):
        return (layer, r - jnp.clip(r - kh // 2, 0, rows - kh), 0, 0, 0)

    return pl.pallas_call(
        functools.partial(_nbr_attn_kernel, head_dim=head_dim, rows=rows, kh=kh),
        out_shape=jax.ShapeDtypeStruct((m, width), BF16),
        grid=(n_seq, rows),
        in_specs=[
            pl.BlockSpec((GRID_W, width), lambda b, r: (b * rows + r, col_block_q)),
            pl.BlockSpec((seq_len, width), lambda b, r: (b, col_block_q + 1)),
            pl.BlockSpec((seq_len, width), lambda b, r: (b, col_block_q + 2)),
            pl.BlockSpec((None, None, past, width), lambda b, r: (b, layer, 0, 0)),
            pl.BlockSpec((None, None, past, width), lambda b, r: (b, layer, 0, 0)),
            pl.BlockSpec((None, None, heads, GRID_W, kh * GRID_W), bias_map),
        ],
        out_specs=pl.BlockSpec((GRID_W, width), lambda b, r: (b * rows + r, 0)),
        compiler_params=_params("parallel", "parallel"),
        name="nbr_attn",
    )(u, u, u, ctx_k, ctx_v, bias)


def _out_proj_kernel(y0_ref, y1_ref, y2_ref, y3_ref, w_ref, x_ref, mod_ref, g_ref, o_ref, *, d, gw):
    mix = None
    for gi, y_ref in enumerate((y0_ref, y1_ref, y2_ref, y3_ref)):
        part = jnp.dot(y_ref[...], w_ref[gi * gw:(gi + 1) * gw, :], preferred_element_type=F32)
        mix = part if mix is None else mix + part
    gate = mod_ref[:, 2 * d:3 * d]
    o_ref[...] = x_ref[...] + gate * _rms(mix, g_ref[...])


def out_proj(ys, w_out, x, mods, g_post, layer, seq_len, cond_base, cond_stride, tm=512):
    m, d = x.shape
    gw = ys[0].shape[-1]
    cidx = functools.partial(_cond_index, tm=tm, seq_len=seq_len, cond_base=cond_base, cond_stride=cond_stride)
    yspec = pl.BlockSpec((tm, gw), lambda i: (i, 0))
    return pl.pallas_call(
        functools.partial(_out_proj_kernel, d=d, gw=gw),
        out_shape=jax.ShapeDtypeStruct((m, d), F32),
        grid=(m // tm,),
        in_specs=[
            yspec, yspec, yspec, yspec,
            pl.BlockSpec((None, w_out.shape[1], d), lambda i: (layer, 0, 0)),
            pl.BlockSpec((tm, d), lambda i: (i, 0)),
            pl.BlockSpec((None, None, 1, N_MOD * d), lambda i: (layer, cidx(i), 0, 0)),
            pl.BlockSpec((None, 1, d), lambda i: (layer, 0, 0)),
        ],
        out_specs=pl.BlockSpec((tm, d), lambda i: (i, 0)),
        compiler_params=_params("parallel"),
        name="out_proj",
    )(*ys, w_out, x, mods, g_post)


def _ffn_kernel(x_ref, mod_ref, gpre_ref, gpost_ref, w1_ref, w3_ref, w2_ref, o_ref, h_ref, acc_ref, *, d):
    j = pl.program_id(1)

    @pl.when(j == 0)
    def _():
        y = _rms(x_ref[...], gpre_ref[...])
        shift = mod_ref[:, 3 * d:4 * d]
        scale = mod_ref[:, 4 * d:5 * d]
        h_ref[...] = (y * (1.0 + scale) + shift).astype(BF16)

    h = h_ref[...]
    a = jnp.dot(h, w1_ref[...], preferred_element_type=F32)
    b = jnp.dot(h, w3_ref[...], preferred_element_type=F32)
    hid = (a * _sigmoid(a) * b).astype(BF16)
    part = jnp.dot(hid, w2_ref[...], preferred_element_type=F32)

    @pl.when(j == 0)
    def _():
        acc_ref[...] = part

    @pl.when(j > 0)
    def _():
        acc_ref[...] += part

    @pl.when(j == pl.num_programs(1) - 1)
    def _():
        gate = mod_ref[:, 5 * d:6 * d]
        o_ref[...] = x_ref[...] + gate * _rms(acc_ref[...], gpost_ref[...])


def ffn(x, mods, g_pre, g_post, w1, w3, w2, layer, seq_len, cond_base, cond_stride, tm=512, tf=512):
    m, d = x.shape
    f = w1.shape[-1]
    cidx = functools.partial(_cond_index, tm=tm, seq_len=seq_len, cond_base=cond_base, cond_stride=cond_stride)
    return pl.pallas_call(
        functools.partial(_ffn_kernel, d=d),
        out_shape=jax.ShapeDtypeStruct((m, d), F32),
        grid=(m // tm, f // tf),
        in_specs=[
            pl.BlockSpec((tm, d), lambda i, j: (i, 0)),
            pl.BlockSpec((None, None, 1, N_MOD * d), lambda i, j: (layer, cidx(i), 0, 0)),
            pl.BlockSpec((None, 1, d), lambda i, j: (layer, 0, 0)),
            pl.BlockSpec((None, 1, d), lambda i, j: (layer, 0, 0)),
            pl.BlockSpec((None, d, tf), lambda i, j: (layer, 0, j)),
            pl.BlockSpec((None, d, tf), lambda i, j: (layer, 0, j)),
            pl.BlockSpec((None, tf, d), lambda i, j: (layer, j, 0)),
        ],
        out_specs=pl.BlockSpec((tm, d), lambda i, j: (i, 0)),
        scratch_shapes=[pltpu.VMEM((tm, d), BF16), pltpu.VMEM((tm, d), F32)],
        compiler_params=_params("parallel", "arbitrary"),
        name="ffn",
    )(x, mods, g_pre, g_post, w1, w3, w2)


def _block_diag_gates(gate_w):
    depth, _, _, nblk, c, _ = gate_w.shape
    per = LANES // c
    w = gate_w.reshape(depth, 4, nblk // per, per, c, c).astype(BF16)
    eye = jnp.eye(per, dtype=BF16)
    tiles = jnp.einsum('lgnpce,pq->lgnpcqe', w, eye)
    return tiles.reshape(depth, 4, nblk // per, LANES, LANES)


def _neighbourhood_bias(tiles, kh):
    depth, heads, _, w, _ = tiles.shape
    per_delta = []
    for delta in range(kh):
        sel = tiles[:, :, NA_KH - 1 - delta:NA_KH - 1 - delta + kh]
        per_delta.append(jnp.transpose(sel, (0, 1, 3, 2, 4)).reshape(depth, heads, w, kh * w))
    return jnp.stack(per_delta, axis=1)


def kernel(x_prompt, x_sample, cache_k, cache_v, state_lru, c, c_ctx, w_ada, b_ada, g_pre_mix, g_post_mix,
           g_pre_ffn, g_post_ffn, w_in, pool_w, pool_scale, lru_conv_w, lru_conv_b, lru_gate_w, lru_gate_b,
           lru_lambda, cm_dw_w, cm_dw_b, cm_ln_g, cm_ln_b, cm_pw_w, cm_pw_b, na_rpb, w_out, ffn_w1, ffn_w3, ffn_w2):
    batch, seq, d = x_prompt.shape
    dec_batch, dec_seq, _ = x_sample.shape
    depth = w_in.shape[0]
    gw = pool_scale.shape[-1]
    heads, head_dim = cache_k.shape[-2], cache_k.shape[-1]
    past = cache_k.shape[2]
    assert dec_batch + 1 <= 8 and seq % SEQ_CHUNK == 0 and dec_seq % SEQ_CHUNK == 0

    cond = jnp.zeros((8, d), F32).at[0].set(c_ctx).at[1:1 + dec_batch].set(c)
    mods = adaln_all(cond, w_ada, b_ada).reshape(depth, 8, 1, N_MOD * d)

    vec = lambda a: a.reshape(depth, 1, a.shape[-1])
    g_pre_mix, g_post_mix, g_pre_ffn, g_post_ffn = map(vec, (g_pre_mix, g_post_mix, g_pre_ffn, g_post_ffn))
    pool_scale, lru_conv_b, cm_dw_b, cm_ln_g, cm_ln_b, cm_pw_b = map(
        vec, (pool_scale, lru_conv_b, cm_dw_b, cm_ln_g, cm_ln_b, cm_pw_b))
    w_in_b, w_out_b, w1_b, w3_b, w2_b = (w.astype(BF16) for w in (w_in, w_out, ffn_w1, ffn_w3, ffn_w2))
    pool_w_b, cm_pw_b16 = pool_w.astype(BF16), cm_pw_w.astype(BF16)
    gate_tiles = _block_diag_gates(lru_gate_w)
    gate_b = lru_gate_b.reshape(depth, 4, gw)
    rows = dec_seq // GRID_W
    kh = min(NA_KH, rows)
    bias = _neighbourhood_bias(rpb_tiles(na_rpb), kh)
    ctx_k = cache_k.reshape(dec_batch, depth, past, gw)
    ctx_v = cache_v.reshape(dec_batch, depth, past, gw)

    cb_pool, cb_lru_x, cb_lru_g, cb_cm, cb_q = 0, 1, 2, 3, 5

    def layer_step(x, l, seq_len, cond_base, cond_stride, h0, latent):
        u = in_proj(x, mods, g_pre_mix, w_in_b, l, seq_len, cond_base, cond_stride)
        y_pool = pool_mixer(u, pool_w_b, pool_scale, l, seq_len, cb_pool)
        y_lru, st = lru_mixer(u, lru_conv_w, lru_conv_b, gate_tiles, gate_b, lru_lambda, h0, l, seq_len,
                              cb_lru_x, cb_lru_g)
        y_cm = conformer(u, cm_dw_w, cm_dw_b, cm_ln_g, cm_ln_b, cm_pw_b16, cm_pw_b, l, seq_len, cb_cm)
        if latent:
            y_att = neighbourhood_attention(u, ctx_k, ctx_v, bias, l, seq_len, cb_q, gw, head_dim)
        else:
            y_att = context_attention(u, seq_len, cb_q, gw, head_dim)
        x = out_proj((y_pool, y_lru, y_cm, y_att), w_out_b, x, mods, g_post_mix, l, seq_len, cond_base, cond_stride)
        x = ffn(x, mods, g_pre_ffn, g_post_ffn, w1_b, w3_b, w2_b, l, seq_len, cond_base, cond_stride)
        return x, u, st

    xp = x_prompt.reshape(batch * seq, d)
    xs = x_sample.reshape(dec_batch * dec_seq, d)
    zeros_h0 = jnp.zeros((batch, 2, gw), F32)
    ks, vs, sts = [], [], []
    for l in range(depth):
        xp, u_p, st = layer_step(xp, l, seq, 0, 0, zeros_h0, latent=False)
        ks.append(u_p[:, (cb_q + 1) * gw:(cb_q + 2) * gw].reshape(batch, seq, heads, head_dim))
        vs.append(u_p[:, (cb_q + 2) * gw:(cb_q + 3) * gw].reshape(batch, seq, heads, head_dim))
        sts.append(st)
        xs, _, _ = layer_step(xs, l, dec_seq, 1, 1, state_lru[:, l], latent=True)

    return (xp.reshape(batch, seq, d), xs.reshape(dec_batch, dec_seq, d),
            jnp.stack(ks, axis=1), jnp.stack(vs, axis=1), jnp.stack(sts, axis=1))
```

```python
import functools

import jax
import jax.numpy as jnp
from jax import lax
from jax.experimental import pallas as pl
from jax.experimental.pallas import tpu as pltpu

F32 = jnp.float32
BF16 = jnp.bfloat16

EPS = 1e-6
NEG_INF = -1e30
N_MOD = 6
POOL_WINDOWS = (2, 4, 8, 16)
LRU_CONV = 4
LRU_C = 8.0
CM_CONV = 31
NA_KH = 8
NA_KW = 16
GRID_W = 64

LANES = 128
SUBLANES = 8
VMEM_LIMIT_BYTES = 60 * 1024 * 1024
SEQ_CHUNK = 256
HALO = 16


def _params(*sem):
    return pltpu.CompilerParams(dimension_semantics=sem, vmem_limit_bytes=VMEM_LIMIT_BYTES)


def _sigmoid(x):
    return 0.5 * jnp.tanh(0.5 * x) + 0.5


def _adaln_kernel(cond_ref, w_ref, b_ref, o_ref):
    c = cond_ref[...]
    s = (c * _sigmoid(c)).astype(BF16)
    o_ref[...] = jnp.dot(s, w_ref[...].astype(BF16), preferred_element_type=F32) + b_ref[...]


def adaln_all(cond, w_ada, b_ada, tn=1024):
    depth, d, n = w_ada.shape
    return pl.pallas_call(
        _adaln_kernel,
        out_shape=jax.ShapeDtypeStruct((depth, 8, n), F32),
        grid=(depth, n // tn),
        in_specs=[
            pl.BlockSpec((8, d), lambda l, j: (0, 0)),
            pl.BlockSpec((None, d, tn), lambda l, j: (l, 0, j)),
            pl.BlockSpec((None, 1, tn), lambda l, j: (l, 0, j)),
        ],
        out_specs=pl.BlockSpec((None, 8, tn), lambda l, j: (l, 0, j)),
        compiler_params=_params("parallel", "parallel"),
        name="adaln",
    )(cond, w_ada, b_ada.reshape(depth, 1, n))


ROW_CHUNK = 128


def _norm_modulate(x_ref, g, shift, scale, h_ref):
    gs = g * (1.0 + scale)
    for r in range(x_ref.shape[0] // ROW_CHUNK):
        rows = slice(r * ROW_CHUNK, (r + 1) * ROW_CHUNK)
        x = x_ref[rows, :]
        rs = lax.rsqrt(jnp.mean(x * x, axis=-1, keepdims=True) + EPS)
        h_ref[rows, :] = (x * rs * gs + shift).astype(h_ref.dtype)


def _gated_residual(x_ref, y_ref, g, gate, o_ref):
    gg = g * gate
    for r in range(x_ref.shape[0] // ROW_CHUNK):
        rows = slice(r * ROW_CHUNK, (r + 1) * ROW_CHUNK)
        y = y_ref[rows, :]
        rs = lax.rsqrt(jnp.mean(y * y, axis=-1, keepdims=True) + EPS)
        o_ref[rows, :] = x_ref[rows, :] + y * rs * gg


def _in_proj_kernel(x_ref, mod_ref, g_ref, w_ref, *rest, d, with_kv):
    if with_kv:
        _, _, u_ref, k_ref, v_ref, h_ref = rest
    else:
        u_ref, h_ref = rest
    j = pl.program_id(1)
    last = pl.num_programs(1) - 1

    def project():
        return jnp.dot(h_ref[...], w_ref[...], preferred_element_type=F32)

    @pl.when(j == 0)
    def _():
        _norm_modulate(x_ref, g_ref[...], mod_ref[:, 0:d], mod_ref[:, d:2 * d], h_ref)
        u_ref[...] = project().astype(u_ref.dtype)

    @pl.when((j > 0) & (j < last))
    def _():
        u_ref[...] = project().astype(u_ref.dtype)

    @pl.when(j == last)
    def _():
        res = project()
        u_ref[...] = res.astype(u_ref.dtype)
        if with_kv:
            half = res.shape[-1] // 2
            k_ref[...] = res[:, :half].reshape(k_ref.shape)
            v_ref[...] = res[:, half:].reshape(v_ref.shape)


def _cond_index(i, tm, seq_len, cond_base, cond_stride):
    assert cond_stride == 0 or seq_len % tm == 0, "a row tile must not straddle two conditioning vectors"
    return cond_base + ((i * tm) // seq_len) * cond_stride


def in_proj(x, mods, g_pre, w_in, layer, seq_len, cond_base, cond_stride, kv=None, tm=1024, tn=1024):
    m, d = x.shape
    n = w_in.shape[-1]
    tm = min(tm, m)
    cidx = functools.partial(_cond_index, tm=tm, seq_len=seq_len, cond_base=cond_base, cond_stride=cond_stride)
    in_specs = [
        pl.BlockSpec((tm, d), lambda i, j: (i, 0)),
        pl.BlockSpec((None, None, 1, N_MOD * d), lambda i, j: (layer, cidx(i), 0, 0)),
        pl.BlockSpec((None, 1, d), lambda i, j: (layer, 0, 0)),
        pl.BlockSpec((None, d, tn), lambda i, j: (layer, 0, j)),
    ]
    out_shape = [jax.ShapeDtypeStruct((m, n), BF16)]
    out_specs = [pl.BlockSpec((tm, tn), lambda i, j: (i, j))]
    args = [x, mods, g_pre, w_in]
    aliases = {}
    if kv is not None:
        width = kv[0].shape[-1]
        assert tn == 2 * width and tm % seq_len == 0
        kv_spec = pl.BlockSpec((tm // seq_len, None, seq_len, width), lambda i, j: (i, layer, 0, 0))
        in_specs += [pl.BlockSpec(memory_space=pl.ANY)] * 2
        out_shape += [jax.ShapeDtypeStruct(kv[0].shape, F32)] * 2
        out_specs += [kv_spec, kv_spec]
        args += list(kv)
        aliases = {4: 1, 5: 2}
    return pl.pallas_call(
        functools.partial(_in_proj_kernel, d=d, with_kv=kv is not None),
        out_shape=out_shape,
        grid=(m // tm, n // tn),
        in_specs=in_specs,
        out_specs=out_specs,
        scratch_shapes=[pltpu.VMEM((tm, d), BF16)],
        input_output_aliases=aliases,
        compiler_params=_params("parallel", "arbitrary"),
        name="in_proj",
    )(*args)


def _halo_maps(col_block, seq_len, n_rows):
    chunks = seq_len // SEQ_CHUNK
    per = SEQ_CHUNK // HALO
    nblk = n_rows // HALO

    def prev_map(s, c):
        return (jnp.maximum((s * chunks + c) * per - 1, 0), col_block)

    def next_map(s, c):
        return (jnp.minimum((s * chunks + c + 1) * per, nblk - 1), col_block)

    return prev_map, next_map


def _pool_kernel(prev_ref, cur_ref, next_ref, w_ref, sc_ref, o_ref, pad_ref, *, chunks, seq_len):
    c = pl.program_id(1)
    cur = cur_ref[...].astype(F32)
    pad_ref[0:HALO, :] = jnp.where(c == 0, 0.0, prev_ref[...].astype(F32))
    pad_ref[HALO:HALO + SEQ_CHUNK, :] = cur
    pad_ref[HALO + SEQ_CHUNK:2 * HALO + SEQ_CHUNK, :] = jnp.where(c == chunks - 1, 0.0, next_ref[...].astype(F32))
    t = c * SEQ_CHUNK + lax.broadcasted_iota(jnp.int32, (SEQ_CHUNK, LANES), 0)
    for gi, w in enumerate(POOL_WINDOWS):
        cols = slice(gi * LANES, (gi + 1) * LANES)
        acc = pad_ref[HALO - w // 2:HALO - w // 2 + SEQ_CHUNK, cols]
        for off in range(-(w // 2) + 1, w - w // 2):
            acc = acc + pad_ref[HALO + off:HALO + off + SEQ_CHUNK, cols]
        lo = jnp.clip(t - w // 2, 0, seq_len)
        hi = jnp.clip(t - w // 2 + w, 0, seq_len)
        pooled = acc / (hi - lo).astype(F32) - cur[:, cols]
        y = jnp.dot(pooled.astype(BF16), w_ref[gi], preferred_element_type=F32) * sc_ref[:, cols]
        o_ref[:, cols] = y.astype(o_ref.dtype)


def pool_mixer(u, pool_w, pool_scale, layer, seq_len, col_block):
    m = u.shape[0]
    width = len(POOL_WINDOWS) * LANES
    n_seq = m // seq_len
    chunks = seq_len // SEQ_CHUNK
    prev_map, next_map = _halo_maps(col_block, seq_len, m)
    return pl.pallas_call(
        functools.partial(_pool_kernel, chunks=chunks, seq_len=seq_len),
        out_shape=jax.ShapeDtypeStruct((m, width), BF16),
        grid=(n_seq, chunks),
        in_specs=[
            pl.BlockSpec((HALO, width), prev_map),
            pl.BlockSpec((SEQ_CHUNK, width), lambda s, c: (s * chunks + c, col_block)),
            pl.BlockSpec((HALO, width), next_map),
            pl.BlockSpec((None, len(POOL_WINDOWS), LANES, LANES), lambda s, c: (layer, 0, 0, 0)),
            pl.BlockSpec((None, 1, width), lambda s, c: (layer, 0, 0)),
        ],
        out_specs=pl.BlockSpec((SEQ_CHUNK, width), lambda s, c: (s * chunks + c, 0)),
        scratch_shapes=[pltpu.VMEM((SEQ_CHUNK + 2 * HALO, width), F32)],
        compiler_params=_params("parallel", "parallel"),
        name="pool",
    )(u, u, u, pool_w, pool_scale)


def _conformer_kernel(ap_ref, a_ref, an_ref, gp_ref, g_ref, gn_ref, dw_ref, dwb_ref, lng_ref, lnb_ref,
                      pw_ref, pwb_ref, o_ref, pad_ref, sh_ref, *, chunks):
    c = pl.program_id(1)
    glu = lambda a, g: a[...].astype(F32) * _sigmoid(g[...].astype(F32))
    pad_ref[0:HALO, :] = jnp.where(c == 0, 0.0, glu(ap_ref, gp_ref))
    pad_ref[HALO:HALO + SEQ_CHUNK, :] = glu(a_ref, g_ref)
    pad_ref[HALO + SEQ_CHUNK:2 * HALO + SEQ_CHUNK, :] = jnp.where(c == chunks - 1, 0.0, glu(an_ref, gn_ref))
    n_sh = sh_ref.shape[1]
    for r in range(1, SUBLANES):
        sh_ref[r - 1] = pad_ref[r:r + n_sh, :]
    first = HALO - CM_CONV // 2
    acc = None
    for k in range(CM_CONV):
        q, r = divmod(first + k, SUBLANES)
        rows = slice(q * SUBLANES, q * SUBLANES + SEQ_CHUNK)
        tap = dw_ref[k:k + 1, :] * (pad_ref[rows, :] if r == 0 else sh_ref[r - 1, rows, :])
        acc = tap + dwb_ref[...] if acc is None else acc + tap
    mu = jnp.mean(acc, axis=-1, keepdims=True)
    cen = acc - mu
    var = jnp.mean(cen * cen, axis=-1, keepdims=True)
    y = cen * lax.rsqrt(var + EPS) * lng_ref[...] + lnb_ref[...]
    z = (y * _sigmoid(y)).astype(BF16)
    o = jnp.dot(z, pw_ref[...], preferred_element_type=F32) + pwb_ref[...]
    o_ref[...] = o.astype(o_ref.dtype)


def conformer(u, dw_w, dw_b, ln_g, ln_b, pw_w, pw_b, layer, seq_len, col_block_a):
    m = u.shape[0]
    width = dw_w.shape[-1]
    n_seq = m // seq_len
    chunks = seq_len // SEQ_CHUNK
    prev_a, next_a = _halo_maps(col_block_a, seq_len, m)
    prev_g, next_g = _halo_maps(col_block_a + 1, seq_len, m)
    vec = lambda: pl.BlockSpec((None, 1, width), lambda s, c: (layer, 0, 0))
    padded = SEQ_CHUNK + 2 * HALO
    return pl.pallas_call(
        functools.partial(_conformer_kernel, chunks=chunks),
        out_shape=jax.ShapeDtypeStruct((m, width), BF16),
        grid=(n_seq, chunks),
        in_specs=[
            pl.BlockSpec((HALO, width), prev_a),
            pl.BlockSpec((SEQ_CHUNK, width), lambda s, c: (s * chunks + c, col_block_a)),
            pl.BlockSpec((HALO, width), next_a),
            pl.BlockSpec((HALO, width), prev_g),
            pl.BlockSpec((SEQ_CHUNK, width), lambda s, c: (s * chunks + c, col_block_a + 1)),
            pl.BlockSpec((HALO, width), next_g),
            pl.BlockSpec((None, CM_CONV, width), lambda s, c: (layer, 0, 0)),
            vec(), vec(), vec(),
            pl.BlockSpec((None, width, width), lambda s, c: (layer, 0, 0)),
            vec(),
        ],
        out_specs=pl.BlockSpec((SEQ_CHUNK, width), lambda s, c: (s * chunks + c, 0)),
        scratch_shapes=[pltpu.VMEM((padded, width), F32),
                        pltpu.VMEM((SUBLANES - 1, padded - SUBLANES, width), F32)],
        compiler_params=_params("parallel", "parallel"),
        name="conformer",
    )(u, u, u, u, u, u, dw_w, dw_b, ln_g, ln_b, pw_w, pw_b)


def _softplus(z):
    return jnp.maximum(z, 0.0) + jnp.log1p(jnp.exp(-jnp.abs(z)))


def _gelu_tanh(x):
    return 0.5 * x * (1.0 + jnp.tanh(0.7978845608028654 * (x + 0.044715 * (x * x * x))))


def _row_select(rows):
    sub = lax.broadcasted_iota(jnp.int32, (SUBLANES, LANES), 0)
    out = jnp.zeros((SUBLANES, LANES), F32)
    for s, r in enumerate(rows):
        out = jnp.where(sub == s, r, out)
    return out


def _lru_kernel(x_ref, g_ref, cw_ref, cb_ref, gw_ref, gb_ref, lam_ref, h0_ref, y_ref, st_ref,
                xpad, a_f, b_f, a_b, b_b, *, seq_len):
    t_len = seq_len
    nb = x_ref.shape[-1] // LANES
    seg = t_len // SUBLANES
    pitch = seg + 4
    width = x_ref.shape[-1]
    xpad[0:SUBLANES, :] = jnp.zeros((SUBLANES, width), F32)
    xpad[SUBLANES + t_len:2 * SUBLANES + t_len, :] = jnp.zeros((SUBLANES, width), F32)
    xpad[SUBLANES:SUBLANES + t_len, :] = x_ref[...].astype(F32)
    left = LRU_CONV // 2
    sp = _softplus(-lam_ref[...])
    for lb in range(nb):
        cols = slice(lb * LANES, (lb + 1) * LANES)
        x = cb_ref[:, cols] + cw_ref[0:1, cols] * xpad[SUBLANES - left:SUBLANES - left + t_len, cols]
        for k in range(1, LRU_CONV):
            x = x + cw_ref[k:k + 1, cols] * xpad[SUBLANES - left + k:SUBLANES - left + k + t_len, cols]
        xb = x.astype(BF16)
        for d, (a_ref, b_ref) in enumerate(((a_f, b_f), (a_b, b_b))):
            gate = lambda gi: _sigmoid(jnp.dot(xb, gw_ref[gi, lb], preferred_element_type=F32) + gb_ref[gi:gi + 1, cols])
            r, i = gate(2 * d), gate(2 * d + 1)
            log_a = (-LRU_C) * r * sp[d:d + 1, cols]
            a = jnp.exp(log_a)
            bt = jnp.sqrt(jnp.maximum(-jnp.tanh(log_a) * (a * a + 1.0), 0.0)) * (i * x)
            for s in range(SUBLANES):
                a_ref[lb, s * pitch:s * pitch + seg, :] = a[s * seg:(s + 1) * seg, :]
                b_ref[lb, s * pitch:s * pitch + seg, :] = bt[s * seg:(s + 1) * seg, :]

    def rows(ref, lb, j):
        return ref[lb, pl.ds(j, SUBLANES, stride=pitch), :]

    zeros = jnp.zeros((SUBLANES, LANES), F32)
    ones = jnp.ones((SUBLANES, LANES), F32)

    def totals(j, carry):
        out = []
        jb = seg - 1 - j
        for lb in range(nb):
            hf, pf, hb, pb = carry[4 * lb:4 * lb + 4]
            af, ab = rows(a_f, lb, j), rows(a_b, lb, jb)
            out += [af * hf + rows(b_f, lb, j), af * pf, ab * hb + rows(b_b, lb, jb), ab * pb]
        return tuple(out)

    tot = lax.fori_loop(0, seg, totals, (zeros, ones, zeros, ones) * nb)

    cins = []
    for lb in range(nb):
        cols = slice(lb * LANES, (lb + 1) * LANES)
        hf, pf, hb, pb = tot[4 * lb:4 * lb + 4]
        cf = [h0_ref[0:1, cols]]
        for s in range(SUBLANES):
            cf.append(hf[s:s + 1, :] + pf[s:s + 1, :] * cf[-1])
        cb = [h0_ref[1:2, cols]]
        for s in reversed(range(SUBLANES)):
            cb.append(hb[s:s + 1, :] + pb[s:s + 1, :] * cb[-1])
        st_ref[0:1, cols] = cf[SUBLANES]
        st_ref[1:2, cols] = cb[SUBLANES]
        cins += [_row_select(cf[:SUBLANES]), _row_select(list(reversed(cb[:SUBLANES])))]

    def states(j, carry):
        out = []
        jb = seg - 1 - j
        for lb in range(nb):
            hf = rows(a_f, lb, j) * carry[2 * lb] + rows(b_f, lb, j)
            hb = rows(a_b, lb, jb) * carry[2 * lb + 1] + rows(b_b, lb, jb)
            b_f[lb, pl.ds(j, SUBLANES, stride=pitch), :] = hf
            b_b[lb, pl.ds(jb, SUBLANES, stride=pitch), :] = hb
            out += [hf, hb]
        return tuple(out)

    lax.fori_loop(0, seg, states, tuple(cins))

    for lb in range(nb):
        cols = slice(lb * LANES, (lb + 1) * LANES)
        for s in range(SUBLANES):
            h = b_f[lb, s * pitch:s * pitch + seg, :] + b_b[lb, s * pitch:s * pitch + seg, :]
            gate = _gelu_tanh(g_ref[s * seg:(s + 1) * seg, cols].astype(F32))
            y_ref[s * seg:(s + 1) * seg, cols] = (h * gate).astype(y_ref.dtype)


def lru_mixer(u, conv_w, conv_b, gate_w, gate_b, lam, h0, layer, seq_len, col_block_x, col_block_g):
    m = u.shape[0]
    width = conv_w.shape[-1]
    nb = width // LANES
    n_seq = m // seq_len
    scan_buf = pltpu.VMEM((nb, SUBLANES * (seq_len // SUBLANES + 4), LANES), F32)
    return pl.pallas_call(
        functools.partial(_lru_kernel, seq_len=seq_len),
        out_shape=(jax.ShapeDtypeStruct((m, width), BF16), jax.ShapeDtypeStruct((n_seq, 2, width), F32)),
        grid=(n_seq,),
        in_specs=[
            pl.BlockSpec((seq_len, width), lambda s: (s, col_block_x)),
            pl.BlockSpec((seq_len, width), lambda s: (s, col_block_g)),
            pl.BlockSpec((None, LRU_CONV, width), lambda s: (layer, 0, 0)),
            pl.BlockSpec((None, 1, width), lambda s: (layer, 0, 0)),
            pl.BlockSpec((None, 4, nb, LANES, LANES), lambda s: (layer, 0, 0, 0, 0)),
            pl.BlockSpec((None, 4, width), lambda s: (layer, 0, 0)),
            pl.BlockSpec((None, 2, width), lambda s: (layer, 0, 0)),
            pl.BlockSpec((None, 2, width), lambda s: (s, 0, 0)),
        ],
        out_specs=(
            pl.BlockSpec((seq_len, width), lambda s: (s, 0)),
            pl.BlockSpec((None, 2, width), lambda s: (s, 0, 0)),
        ),
        scratch_shapes=[pltpu.VMEM((seq_len + 2 * SUBLANES, width), F32), scan_buf, scan_buf, scan_buf, scan_buf],
        compiler_params=_params("parallel"),
        name="lru",
    )(u, u, conv_w, conv_b, gate_w, gate_b, lam, h0)


def _head_masks():
    lane = lax.broadcasted_iota(jnp.int32, (1, LANES), 1)
    return lane < LANES // 2, lane >= LANES // 2


def _nt_dot(a, b):
    return lax.dot_general(a, b, (((1,), (1,)), ((), ())), preferred_element_type=F32)


def _scaled(q, scale):
    return (q.astype(F32) * scale).astype(BF16)


def _ctx_attn_kernel(q_ref, k_ref, v_ref, o_ref, *, head_dim):
    masks = _head_masks()
    zero = jnp.zeros((), BF16)
    for p in range(q_ref.shape[-1] // LANES):
        cols = slice(p * LANES, (p + 1) * LANES)
        q = _scaled(q_ref[:, cols], head_dim ** -0.5)
        k = k_ref[:, cols]
        v = v_ref[:, cols]
        out = None
        for msk in masks:
            s = _nt_dot(jnp.where(msk, q, zero), k)
            e = jnp.exp(s - jnp.max(s, axis=-1, keepdims=True))
            den = jnp.sum(e, axis=-1, keepdims=True)
            o = jnp.dot(e.astype(BF16), jnp.where(msk, v, zero), preferred_element_type=F32) / den
            out = o if out is None else out + o
        o_ref[:, cols] = out.astype(o_ref.dtype)


def context_attention(u, seq_len, col_block_q, width, head_dim):
    m = u.shape[0]
    n_seq = m // seq_len
    spec = lambda cb: pl.BlockSpec((seq_len, width), lambda s: (s, cb))
    return pl.pallas_call(
        functools.partial(_ctx_attn_kernel, head_dim=head_dim),
        out_shape=jax.ShapeDtypeStruct((m, width), BF16),
        grid=(n_seq,),
        in_specs=[spec(col_block_q), spec(col_block_q + 1), spec(col_block_q + 2)],
        out_specs=pl.BlockSpec((seq_len, width), lambda s: (s, 0)),
        compiler_params=_params("parallel"),
        name="ctx_attn",
    )(u, u, u)


def _rpb_tile_kernel(rpb_ref, o_ref):
    n_rows, nd = 2 * NA_KH - 1, 2 * NA_KW - 1
    qc = lax.broadcasted_iota(jnp.int32, (GRID_W, 2 * GRID_W), 0)
    lane = lax.broadcasted_iota(jnp.int32, (GRID_W, 2 * GRID_W), 1)
    kc = lane & (GRID_W - 1)
    start = jnp.clip(qc - NA_KW // 2, 0, GRID_W - NA_KW)
    inside = (kc >= start) & (kc < start + NA_KW)
    idx = jnp.clip(kc - qc + NA_KW - 1, 0, nd - 1)
    base = (pl.program_id(0) * pl.num_programs(1) + pl.program_id(1)) * n_rows
    tiles = []
    for r in range(n_rows):
        acc = jnp.zeros((GRID_W, 2 * GRID_W), F32)
        for dd in range(nd):
            acc = jnp.where(idx == dd, rpb_ref[(base + r) * nd + dd], acc)
        tiles.append(jnp.where(inside, acc, NEG_INF))
    for r in range(n_rows - 1):
        o_ref[r] = jnp.where(lane < GRID_W, tiles[r], tiles[r + 1])


def rpb_tiles(rpb):
    depth, heads, nr, _ = rpb.shape
    return pl.pallas_call(
        _rpb_tile_kernel,
        out_shape=jax.ShapeDtypeStruct((depth, heads, nr - 1, GRID_W, 2 * GRID_W), F32),
        grid=(depth, heads),
        in_specs=[pl.BlockSpec(memory_space=pltpu.SMEM)],
        out_specs=pl.BlockSpec((None, None, nr - 1, GRID_W, 2 * GRID_W), lambda l, h: (l, h, 0, 0, 0)),
        compiler_params=_params("parallel", "parallel"),
        name="rpb_tiles",
    )(rpb.reshape(-1))


NBR_ROWS = 4


def _nbr_attn_kernel(q_ref, k_ref, v_ref, ck_ref, cv_ref, tile_ref, o_ref, *, head_dim, rows, kh):
    masks = _head_masks()
    zero = jnp.zeros((), BF16)
    w = GRID_W
    windows, deltas = [], []
    for rr in range(NBR_ROWS):
        r = pl.program_id(1) * NBR_ROWS + rr
        r0 = jnp.clip(r - kh // 2, 0, rows - kh)
        windows.append(pl.ds(pl.multiple_of(r0 * w, w), kh * w))
        deltas.append(r - r0)
    for p in range(q_ref.shape[-1] // LANES):
        cols = slice(p * LANES, (p + 1) * LANES)
        q = _scaled(q_ref[:, cols], head_dim ** -0.5)
        ck = ck_ref[:, cols]
        cv = cv_ref[:, cols]
        kw = [k_ref[win, cols] for win in windows]
        vw = [v_ref[win, cols] for win in windows]
        out = None
        for hh, msk in enumerate(masks):
            qm = jnp.where(msk, q, zero)
            s_cx = _nt_dot(qm, ck)
            s_nb = []
            for rr in range(NBR_ROWS):
                first = NA_KH - 1 - deltas[rr]
                bias = jnp.concatenate([tile_ref[2 * p + hh, first + 2 * jj] for jj in range(kh // 2)], axis=-1)
                s_nb.append(_nt_dot(qm[rr * w:(rr + 1) * w, :], kw[rr]) + bias)
            s_nb = jnp.concatenate(s_nb, axis=0)
            mx = jnp.maximum(jnp.max(s_nb, axis=-1, keepdims=True), jnp.max(s_cx, axis=-1, keepdims=True))
            e_nb = jnp.exp(s_nb - mx)
            e_cx = jnp.exp(s_cx - mx)
            den = jnp.sum(e_nb, axis=-1, keepdims=True) + jnp.sum(e_cx, axis=-1, keepdims=True)
            e_nb = e_nb.astype(BF16)
            o_nb = jnp.concatenate(
                [jnp.dot(e_nb[rr * w:(rr + 1) * w, :], jnp.where(msk, vw[rr], zero), preferred_element_type=F32)
                 for rr in range(NBR_ROWS)], axis=0)
            o = jnp.dot(e_cx.astype(BF16), jnp.where(msk, cv, zero), preferred_element_type=F32)
            o = (o + o_nb) / den
            out = o if out is None else out + o
        o_ref[:, cols] = out.astype(o_ref.dtype)


def neighbourhood_attention(u, ctx_k, ctx_v, tiles, layer, seq_len, col_block_q, width, head_dim):
    m = u.shape[0]
    n_seq = m // seq_len
    rows = seq_len // GRID_W
    kh = min(NA_KH, rows)
    assert kh % 2 == 0 and rows % NBR_ROWS == 0
    groups = rows // NBR_ROWS
    past = ctx_k.shape[2]
    return pl.pallas_call(
        functools.partial(_nbr_attn_kernel, head_dim=head_dim, rows=rows, kh=kh),
        out_shape=jax.ShapeDtypeStruct((m, width), BF16),
        grid=(n_seq, groups),
        in_specs=[
            pl.BlockSpec((NBR_ROWS * GRID_W, width), lambda b, g: (b * groups + g, col_block_q)),
            pl.BlockSpec((seq_len, width), lambda b, g: (b, col_block_q + 1)),
            pl.BlockSpec((seq_len, width), lambda b, g: (b, col_block_q + 2)),
            pl.BlockSpec((None, None, past, width), lambda b, g: (b, layer, 0, 0)),
            pl.BlockSpec((None, None, past, width), lambda b, g: (b, layer, 0, 0)),
            pl.BlockSpec((None,) + tiles.shape[1:], lambda b, g: (layer, 0, 0, 0, 0)),
        ],
        out_specs=pl.BlockSpec((NBR_ROWS * GRID_W, width), lambda b, g: (b * groups + g, 0)),
        compiler_params=_params("parallel", "parallel"),
        name="nbr_attn",
    )(u, u, u, ctx_k, ctx_v, tiles)


def _out_proj_kernel(y0_ref, y1_ref, y2_ref, y3_ref, w_ref, x_ref, mod_ref, g_ref, o_ref, *, d, gw):
    gg = g_ref[...] * mod_ref[:, 2 * d:3 * d]
    for r in range(x_ref.shape[0] // ROW_CHUNK):
        rows = slice(r * ROW_CHUNK, (r + 1) * ROW_CHUNK)
        mix = None
        for gi, y_ref in enumerate((y0_ref, y1_ref, y2_ref, y3_ref)):
            part = jnp.dot(y_ref[rows, :], w_ref[gi * gw:(gi + 1) * gw, :], preferred_element_type=F32)
            mix = part if mix is None else mix + part
        rs = lax.rsqrt(jnp.mean(mix * mix, axis=-1, keepdims=True) + EPS)
        o_ref[rows, :] = x_ref[rows, :] + mix * rs * gg


def out_proj(ys, w_out, x, mods, g_post, layer, seq_len, cond_base, cond_stride, tm=512):
    m, d = x.shape
    gw = ys[0].shape[-1]
    tm = min(tm, m)
    cidx = functools.partial(_cond_index, tm=tm, seq_len=seq_len, cond_base=cond_base, cond_stride=cond_stride)
    yspec = pl.BlockSpec((tm, gw), lambda i: (i, 0))
    return pl.pallas_call(
        functools.partial(_out_proj_kernel, d=d, gw=gw),
        out_shape=jax.ShapeDtypeStruct((m, d), F32),
        grid=(m // tm,),
        in_specs=[
            yspec, yspec, yspec, yspec,
            pl.BlockSpec((None, w_out.shape[1], d), lambda i: (layer, 0, 0)),
            pl.BlockSpec((tm, d), lambda i: (i, 0)),
            pl.BlockSpec((None, None, 1, N_MOD * d), lambda i: (layer, cidx(i), 0, 0)),
            pl.BlockSpec((None, 1, d), lambda i: (layer, 0, 0)),
        ],
        out_specs=pl.BlockSpec((tm, d), lambda i: (i, 0)),
        compiler_params=_params("parallel"),
        name="out_proj",
    )(*ys, w_out, x, mods, g_post)


def _ffn_kernel(x_ref, mod_ref, gpre_ref, gpost_ref, w1_ref, w3_ref, w2_ref, o_ref, h_ref, *, d, rb):
    j = pl.program_id(1)

    @pl.when(j == 0)
    def _():
        _norm_modulate(x_ref, gpre_ref[...], mod_ref[:, 3 * d:4 * d], mod_ref[:, 4 * d:5 * d], h_ref)

    def step(first):
        for r in range(h_ref.shape[0] // rb):
            rows = slice(r * rb, (r + 1) * rb)
            h = h_ref[rows, :]
            a = jnp.dot(h, w1_ref[...], preferred_element_type=F32)
            b = jnp.dot(h, w3_ref[...], preferred_element_type=F32)
            hid = (a * _sigmoid(a) * b).astype(BF16)
            part = jnp.dot(hid, w2_ref[...], preferred_element_type=F32)
            if first:
                o_ref[rows, :] = part
            else:
                o_ref[rows, :] += part

    last = pl.num_programs(1) - 1

    @pl.when(j == 0)
    def _():
        step(True)

    @pl.when((j > 0) & (j < last))
    def _():
        step(False)

    @pl.when(j == last)
    def _():
        step(False)
        _gated_residual(x_ref, o_ref, gpost_ref[...], mod_ref[:, 5 * d:6 * d], o_ref)


def ffn(x, mods, g_pre, g_post, w1, w3, w2, layer, seq_len, cond_base, cond_stride, tm=1024, tf=512, rb=512):
    m, d = x.shape
    f = w1.shape[-1]
    tm = min(tm, m)
    rb = min(rb, tm)
    cidx = functools.partial(_cond_index, tm=tm, seq_len=seq_len, cond_base=cond_base, cond_stride=cond_stride)
    return pl.pallas_call(
        functools.partial(_ffn_kernel, d=d, rb=rb),
        out_shape=jax.ShapeDtypeStruct((m, d), F32),
        grid=(m // tm, f // tf),
        in_specs=[
            pl.BlockSpec((tm, d), lambda i, j: (i, 0)),
            pl.BlockSpec((None, None, 1, N_MOD * d), lambda i, j: (layer, cidx(i), 0, 0)),
            pl.BlockSpec((None, 1, d), lambda i, j: (layer, 0, 0)),
            pl.BlockSpec((None, 1, d), lambda i, j: (layer, 0, 0)),
            pl.BlockSpec((None, d, tf), lambda i, j: (layer, 0, j)),
            pl.BlockSpec((None, d, tf), lambda i, j: (layer, 0, j)),
            pl.BlockSpec((None, tf, d), lambda i, j: (layer, j, 0)),
        ],
        out_specs=pl.BlockSpec((tm, d), lambda i, j: (i, 0)),
        scratch_shapes=[pltpu.VMEM((tm, d), BF16)],
        compiler_params=_params("parallel", "arbitrary"),
        name="ffn",
    )(x, mods, g_pre, g_post, w1, w3, w2)


def _block_diag_gates(gate_w):
    depth, _, _, nblk, c, _ = gate_w.shape
    per = LANES // c
    w = gate_w.reshape(depth, 4, nblk // per, per, c, c).astype(BF16)
    eye = jnp.eye(per, dtype=BF16)
    tiles = jnp.einsum('lgnpce,pq->lgnpcqe', w, eye)
    return tiles.reshape(depth, 4, nblk // per, LANES, LANES)


def kernel(x_prompt, x_sample, cache_k, cache_v, state_lru, c, c_ctx, w_ada, b_ada, g_pre_mix, g_post_mix,
           g_pre_ffn, g_post_ffn, w_in, pool_w, pool_scale, lru_conv_w, lru_conv_b, lru_gate_w, lru_gate_b,
           lru_lambda, cm_dw_w, cm_dw_b, cm_ln_g, cm_ln_b, cm_pw_w, cm_pw_b, na_rpb, w_out, ffn_w1, ffn_w3, ffn_w2):
    batch, seq, d = x_prompt.shape
    dec_batch, dec_seq, _ = x_sample.shape
    depth = w_in.shape[0]
    gw = pool_scale.shape[-1]
    heads, head_dim = cache_k.shape[-2], cache_k.shape[-1]
    past = cache_k.shape[2]
    assert dec_batch + 1 <= 8 and seq % SEQ_CHUNK == 0 and dec_seq % SEQ_CHUNK == 0

    cond = jnp.zeros((8, d), F32).at[0].set(c_ctx).at[1:1 + dec_batch].set(c)
    mods = adaln_all(cond, w_ada, b_ada).reshape(depth, 8, 1, N_MOD * d)

    vec = lambda a: a.reshape(depth, 1, a.shape[-1])
    g_pre_mix, g_post_mix, g_pre_ffn, g_post_ffn = map(vec, (g_pre_mix, g_post_mix, g_pre_ffn, g_post_ffn))
    pool_scale, lru_conv_b, cm_dw_b, cm_ln_g, cm_ln_b, cm_pw_b = map(
        vec, (pool_scale, lru_conv_b, cm_dw_b, cm_ln_g, cm_ln_b, cm_pw_b))
    w_in_b, w_out_b, w1_b, w3_b, w2_b = (w.astype(BF16) for w in (w_in, w_out, ffn_w1, ffn_w3, ffn_w2))
    pool_w_b, cm_pw_b16 = pool_w.astype(BF16), cm_pw_w.astype(BF16)
    gate_tiles = _block_diag_gates(lru_gate_w)
    gate_b = lru_gate_b.reshape(depth, 4, gw)
    tiles = rpb_tiles(na_rpb)
    ctx_k = cache_k.reshape(dec_batch, depth, past, gw).astype(BF16)
    ctx_v = cache_v.reshape(dec_batch, depth, past, gw).astype(BF16)

    cb_pool, cb_lru_x, cb_lru_g, cb_cm, cb_q = 0, 1, 2, 3, 5

    def layer_step(x, l, seq_len, cond_base, cond_stride, h0, kv):
        res = in_proj(x, mods, g_pre_mix, w_in_b, l, seq_len, cond_base, cond_stride, kv=kv)
        u = res[0]
        y_pool = pool_mixer(u, pool_w_b, pool_scale, l, seq_len, cb_pool)
        y_lru, st = lru_mixer(u, lru_conv_w, lru_conv_b, gate_tiles, gate_b, lru_lambda, h0, l, seq_len,
                              cb_lru_x, cb_lru_g)
        y_cm = conformer(u, cm_dw_w, cm_dw_b, cm_ln_g, cm_ln_b, cm_pw_b16, cm_pw_b, l, seq_len, cb_cm)
        if kv is None:
            y_att = neighbourhood_attention(u, ctx_k, ctx_v, tiles, l, seq_len, cb_q, gw, head_dim)
        else:
            y_att = context_attention(u, seq_len, cb_q, gw, head_dim)
        x = out_proj((y_pool, y_lru, y_cm, y_att), w_out_b, x, mods, g_post_mix, l, seq_len, cond_base, cond_stride)
        x = ffn(x, mods, g_pre_ffn, g_post_ffn, w1_b, w3_b, w2_b, l, seq_len, cond_base, cond_stride)
        return x, st, tuple(res[1:])

    xp = x_prompt.reshape(batch * seq, d)
    xs = x_sample.reshape(dec_batch * dec_seq, d)
    zeros_h0 = jnp.zeros((batch, 2, gw), F32)
    kv = tuple(jnp.zeros((batch, depth, seq, gw), F32) for _ in range(2))
    sts = []
    for l in range(depth):
        xp, st, kv = layer_step(xp, l, seq, 0, 0, zeros_h0, kv)
        sts.append(st)
        xs, _, _ = layer_step(xs, l, dec_seq, 1, 1, state_lru[:, l], None)

    out_kv = (batch, depth, seq, heads, head_dim)
    return (xp.reshape(batch, seq, d), xs.reshape(dec_batch, dec_seq, d),
            kv[0].reshape(out_kv), kv[1].reshape(out_kv), jnp.stack(sts, axis=1))
```

```python
import functools

import jax
import jax.numpy as jnp
from jax import lax
from jax.experimental import pallas as pl
from jax.experimental.pallas import tpu as pltpu

F32 = jnp.float32
BF16 = jnp.bfloat16

EPS = 1e-6
NEG_INF = -1e30
N_MOD = 6
POOL_WINDOWS = (2, 4, 8, 16)
LRU_CONV = 4
LRU_C = 8.0
CM_CONV = 31
NA_KH = 8
NA_KW = 16
GRID_W = 64

LANES = 128
SUBLANES = 8
VMEM_LIMIT_BYTES = 60 * 1024 * 1024
SEQ_CHUNK = 256
HALO = 16


def _params(*sem):
    return pltpu.CompilerParams(dimension_semantics=sem, vmem_limit_bytes=VMEM_LIMIT_BYTES)


def _sigmoid(x):
    return 0.5 * jnp.tanh(0.5 * x) + 0.5


def _adaln_kernel(cond_ref, w_ref, b_ref, o_ref):
    c = cond_ref[...]
    s = (c * _sigmoid(c)).astype(BF16)
    o_ref[...] = jnp.dot(s, w_ref[...].astype(BF16), preferred_element_type=F32) + b_ref[...]


def adaln_all(cond, w_ada, b_ada, tn=1024):
    depth, d, n = w_ada.shape
    return pl.pallas_call(
        _adaln_kernel,
        out_shape=jax.ShapeDtypeStruct((depth, 8, n), F32),
        grid=(depth, n // tn),
        in_specs=[
            pl.BlockSpec((8, d), lambda l, j: (0, 0)),
            pl.BlockSpec((None, d, tn), lambda l, j: (l, 0, j)),
            pl.BlockSpec((None, 1, tn), lambda l, j: (l, 0, j)),
        ],
        out_specs=pl.BlockSpec((None, 8, tn), lambda l, j: (l, 0, j)),
        compiler_params=_params("parallel", "parallel"),
        name="adaln",
    )(cond, w_ada, b_ada.reshape(depth, 1, n))


ROW_CHUNK = 128


def _norm_modulate(x_ref, g, shift, scale, h_ref):
    gs = g * (1.0 + scale)
    for r in range(x_ref.shape[0] // ROW_CHUNK):
        rows = slice(r * ROW_CHUNK, (r + 1) * ROW_CHUNK)
        x = x_ref[rows, :]
        rs = lax.rsqrt(jnp.mean(x * x, axis=-1, keepdims=True) + EPS)
        h_ref[rows, :] = (x * rs * gs + shift).astype(h_ref.dtype)


def _gated_residual(x_ref, y_ref, g, gate, o_ref):
    gg = g * gate
    for r in range(x_ref.shape[0] // ROW_CHUNK):
        rows = slice(r * ROW_CHUNK, (r + 1) * ROW_CHUNK)
        y = y_ref[rows, :]
        rs = lax.rsqrt(jnp.mean(y * y, axis=-1, keepdims=True) + EPS)
        o_ref[rows, :] = x_ref[rows, :] + y * rs * gg


def _in_proj_kernel(x_ref, mod_ref, g_ref, w_ref, *rest, d, tn, with_kv):
    if with_kv:
        _, _, u_ref, k_ref, v_ref, h_ref = rest
    else:
        u_ref, h_ref = rest
    _norm_modulate(x_ref, g_ref[...], mod_ref[:, 0:d], mod_ref[:, d:2 * d], h_ref)
    n = w_ref.shape[-1]
    for c in range(n // tn):
        res = jnp.dot(h_ref[...], w_ref[:, c * tn:(c + 1) * tn], preferred_element_type=F32)
        u_ref[:, c * tn:(c + 1) * tn] = res.astype(u_ref.dtype)
        if with_kv and c == n // tn - 1:
            half = tn // 2
            k_ref[...] = res[:, :half].reshape(k_ref.shape)
            v_ref[...] = res[:, half:].reshape(v_ref.shape)


def _cond_index(i, tm, seq_len, cond_base, cond_stride):
    assert cond_stride == 0 or seq_len % tm == 0, "a row tile must not straddle two conditioning vectors"
    return cond_base + ((i * tm) // seq_len) * cond_stride


def in_proj(x, mods, g_pre, w_in, layer, seq_len, cond_base, cond_stride, kv=None, tm=512, tn=1024):
    m, d = x.shape
    n = w_in.shape[-1]
    tm = min(tm, m)
    assert n % tn == 0
    cidx = functools.partial(_cond_index, tm=tm, seq_len=seq_len, cond_base=cond_base, cond_stride=cond_stride)
    in_specs = [
        pl.BlockSpec((tm, d), lambda i: (i, 0)),
        pl.BlockSpec((None, None, 1, N_MOD * d), lambda i: (layer, cidx(i), 0, 0)),
        pl.BlockSpec((None, 1, d), lambda i: (layer, 0, 0)),
        pl.BlockSpec((None, d, n), lambda i: (layer, 0, 0), pipeline_mode=pl.Buffered(1)),
    ]
    out_shape = [jax.ShapeDtypeStruct((m, n), BF16)]
    out_specs = [pl.BlockSpec((tm, n), lambda i: (i, 0))]
    args = [x, mods, g_pre, w_in]
    aliases = {}
    if kv is not None:
        width = kv[0].shape[-1]
        assert tn == 2 * width and tm % seq_len == 0
        kv_spec = pl.BlockSpec((tm // seq_len, None, seq_len, width), lambda i: (i, layer, 0, 0))
        in_specs += [pl.BlockSpec(memory_space=pl.ANY)] * 2
        out_shape += [jax.ShapeDtypeStruct(kv[0].shape, F32)] * 2
        out_specs += [kv_spec, kv_spec]
        args += list(kv)
        aliases = {4: 1, 5: 2}
    return pl.pallas_call(
        functools.partial(_in_proj_kernel, d=d, tn=tn, with_kv=kv is not None),
        out_shape=out_shape,
        grid=(m // tm,),
        in_specs=in_specs,
        out_specs=out_specs,
        scratch_shapes=[pltpu.VMEM((tm, d), BF16)],
        input_output_aliases=aliases,
        compiler_params=_params("parallel"),
        name="in_proj",
    )(*args)


def _halo_maps(col_block, seq_len, n_rows):
    chunks = seq_len // SEQ_CHUNK
    per = SEQ_CHUNK // HALO
    nblk = n_rows // HALO

    def prev_map(s, c):
        return (jnp.maximum((s * chunks + c) * per - 1, 0), col_block)

    def next_map(s, c):
        return (jnp.minimum((s * chunks + c + 1) * per, nblk - 1), col_block)

    return prev_map, next_map


def _pool_body(prev_ref, cur_ref, next_ref, w_ref, sc_ref, o_ref, pad_ref, *, chunks, seq_len):
    c = pl.program_id(1)
    cur = cur_ref[...].astype(F32)
    pad_ref[0:HALO, :] = jnp.where(c == 0, 0.0, prev_ref[...].astype(F32))
    pad_ref[HALO:HALO + SEQ_CHUNK, :] = cur
    pad_ref[HALO + SEQ_CHUNK:2 * HALO + SEQ_CHUNK, :] = jnp.where(c == chunks - 1, 0.0, next_ref[...].astype(F32))
    t = c * SEQ_CHUNK + lax.broadcasted_iota(jnp.int32, (SEQ_CHUNK, LANES), 0)
    for gi, w in enumerate(POOL_WINDOWS):
        cols = slice(gi * LANES, (gi + 1) * LANES)
        acc = pad_ref[HALO - w // 2:HALO - w // 2 + SEQ_CHUNK, cols]
        for off in range(-(w // 2) + 1, w - w // 2):
            acc = acc + pad_ref[HALO + off:HALO + off + SEQ_CHUNK, cols]
        lo = jnp.clip(t - w // 2, 0, seq_len)
        hi = jnp.clip(t - w // 2 + w, 0, seq_len)
        pooled = acc / (hi - lo).astype(F32) - cur[:, cols]
        y = jnp.dot(pooled.astype(BF16), w_ref[gi], preferred_element_type=F32) * sc_ref[:, cols]
        o_ref[:, cols] = y.astype(o_ref.dtype)


def _conformer_body(ap_ref, a_ref, an_ref, gp_ref, g_ref, gn_ref, dw_ref, dwb_ref, lng_ref, lnb_ref,
                    pw_ref, pwb_ref, o_ref, pad_ref, sh_ref, *, chunks):
    c = pl.program_id(1)
    glu = lambda a, g: a[...].astype(F32) * _sigmoid(g[...].astype(F32))
    pad_ref[0:HALO, :] = jnp.where(c == 0, 0.0, glu(ap_ref, gp_ref))
    pad_ref[HALO:HALO + SEQ_CHUNK, :] = glu(a_ref, g_ref)
    pad_ref[HALO + SEQ_CHUNK:2 * HALO + SEQ_CHUNK, :] = jnp.where(c == chunks - 1, 0.0, glu(an_ref, gn_ref))
    n_sh = sh_ref.shape[1]
    for r in range(1, SUBLANES):
        sh_ref[r - 1] = pad_ref[r:r + n_sh, :]
    first = HALO - CM_CONV // 2
    acc = None
    for k in range(CM_CONV):
        q, r = divmod(first + k, SUBLANES)
        rows = slice(q * SUBLANES, q * SUBLANES + SEQ_CHUNK)
        tap = dw_ref[k:k + 1, :] * (pad_ref[rows, :] if r == 0 else sh_ref[r - 1, rows, :])
        acc = tap + dwb_ref[...] if acc is None else acc + tap
    mu = jnp.mean(acc, axis=-1, keepdims=True)
    cen = acc - mu
    var = jnp.mean(cen * cen, axis=-1, keepdims=True)
    y = cen * lax.rsqrt(var + EPS) * lng_ref[...] + lnb_ref[...]
    z = (y * _sigmoid(y)).astype(BF16)
    o = jnp.dot(z, pw_ref[...], preferred_element_type=F32) + pwb_ref[...]
    o_ref[...] = o.astype(o_ref.dtype)


def _pool_conformer_kernel(*refs, chunks, seq_len):
    pool_in, cm_in = refs[0:5], refs[5:17]
    pool_out, cm_out, pool_pad, cm_pad, cm_sh = refs[17:22]
    _pool_body(*pool_in, pool_out, pool_pad, chunks=chunks, seq_len=seq_len)
    _conformer_body(*cm_in, cm_out, cm_pad, cm_sh, chunks=chunks)


def pool_conformer(u, pool_w, pool_scale, dw_w, dw_b, ln_g, ln_b, pw_w, pw_b, layer, seq_len, col_block_pool, col_block_a):
    m = u.shape[0]
    width = dw_w.shape[-1]
    assert width == len(POOL_WINDOWS) * LANES
    n_seq = m // seq_len
    chunks = seq_len // SEQ_CHUNK
    padded = SEQ_CHUNK + 2 * HALO
    vec = lambda: pl.BlockSpec((None, 1, width), lambda s, c: (layer, 0, 0))

    def chunk_specs(col_block):
        prev_map, next_map = _halo_maps(col_block, seq_len, m)
        return [pl.BlockSpec((HALO, width), prev_map),
                pl.BlockSpec((SEQ_CHUNK, width), lambda s, c: (s * chunks + c, col_block)),
                pl.BlockSpec((HALO, width), next_map)]

    pa, a, na = chunk_specs(col_block_a)
    pg, g, ng = chunk_specs(col_block_a + 1)
    out_spec = pl.BlockSpec((SEQ_CHUNK, width), lambda s, c: (s * chunks + c, 0))
    return pl.pallas_call(
        functools.partial(_pool_conformer_kernel, chunks=chunks, seq_len=seq_len),
        out_shape=(jax.ShapeDtypeStruct((m, width), BF16),) * 2,
        grid=(n_seq, chunks),
        in_specs=chunk_specs(col_block_pool) + [
            pl.BlockSpec((None, len(POOL_WINDOWS), LANES, LANES), lambda s, c: (layer, 0, 0, 0)),
            vec(),
            pa, a, na, pg, g, ng,
            pl.BlockSpec((None, CM_CONV, width), lambda s, c: (layer, 0, 0)),
            vec(), vec(), vec(),
            pl.BlockSpec((None, width, width), lambda s, c: (layer, 0, 0)),
            vec(),
        ],
        out_specs=(out_spec, out_spec),
        scratch_shapes=[pltpu.VMEM((padded, width), F32), pltpu.VMEM((padded, width), F32),
                        pltpu.VMEM((SUBLANES - 1, padded - SUBLANES, width), F32)],
        compiler_params=_params("parallel", "parallel"),
        name="pool_conformer",
    )(u, u, u, pool_w, pool_scale, u, u, u, u, u, u, dw_w, dw_b, ln_g, ln_b, pw_w, pw_b)


def _softplus(z):
    return jnp.maximum(z, 0.0) + jnp.log1p(jnp.exp(-jnp.abs(z)))


def _gelu_tanh(x):
    return 0.5 * x * (1.0 + jnp.tanh(0.7978845608028654 * (x + 0.044715 * (x * x * x))))


def _row_select(rows):
    sub = lax.broadcasted_iota(jnp.int32, (SUBLANES, LANES), 0)
    out = jnp.zeros((SUBLANES, LANES), F32)
    for s, r in enumerate(rows):
        out = jnp.where(sub == s, r, out)
    return out


def _lru_kernel(x_ref, g_ref, cw_ref, cb_ref, gw_ref, gb_ref, lam_ref, h0_ref, y_ref, st_ref,
                xpad, a_f, b_f, a_b, b_b, *, seq_len):
    t_len = seq_len
    nb = x_ref.shape[-1] // LANES
    seg = t_len // SUBLANES
    pitch = seg + 4
    width = x_ref.shape[-1]
    xpad[0:SUBLANES, :] = jnp.zeros((SUBLANES, width), F32)
    xpad[SUBLANES + t_len:2 * SUBLANES + t_len, :] = jnp.zeros((SUBLANES, width), F32)
    xpad[SUBLANES:SUBLANES + t_len, :] = x_ref[...].astype(F32)
    left = LRU_CONV // 2
    sp = _softplus(-lam_ref[...])
    for lb in range(nb):
        cols = slice(lb * LANES, (lb + 1) * LANES)
        x = cb_ref[:, cols] + cw_ref[0:1, cols] * xpad[SUBLANES - left:SUBLANES - left + t_len, cols]
        for k in range(1, LRU_CONV):
            x = x + cw_ref[k:k + 1, cols] * xpad[SUBLANES - left + k:SUBLANES - left + k + t_len, cols]
        xb = x.astype(BF16)
        for d, (a_ref, b_ref) in enumerate(((a_f, b_f), (a_b, b_b))):
            gate = lambda gi: _sigmoid(jnp.dot(xb, gw_ref[gi, lb], preferred_element_type=F32) + gb_ref[gi:gi + 1, cols])
            r, i = gate(2 * d), gate(2 * d + 1)
            log_a = (-LRU_C) * r * sp[d:d + 1, cols]
            a = jnp.exp(log_a)
            bt = jnp.sqrt(jnp.maximum(-jnp.tanh(log_a) * (a * a + 1.0), 0.0)) * (i * x)
            for s in range(SUBLANES):
                a_ref[lb, s * pitch:s * pitch + seg, :] = a[s * seg:(s + 1) * seg, :]
                b_ref[lb, s * pitch:s * pitch + seg, :] = bt[s * seg:(s + 1) * seg, :]

    def rows(ref, lb, j):
        return ref[lb, pl.ds(j, SUBLANES, stride=pitch), :]

    zeros = jnp.zeros((SUBLANES, LANES), F32)
    ones = jnp.ones((SUBLANES, LANES), F32)

    def totals(j, carry):
        out = []
        jb = seg - 1 - j
        for lb in range(nb):
            hf, pf, hb, pb = carry[4 * lb:4 * lb + 4]
            af, ab = rows(a_f, lb, j), rows(a_b, lb, jb)
            out += [af * hf + rows(b_f, lb, j), af * pf, ab * hb + rows(b_b, lb, jb), ab * pb]
        return tuple(out)

    tot = lax.fori_loop(0, seg, totals, (zeros, ones, zeros, ones) * nb)

    cins = []
    for lb in range(nb):
        cols = slice(lb * LANES, (lb + 1) * LANES)
        hf, pf, hb, pb = tot[4 * lb:4 * lb + 4]
        cf = [h0_ref[0:1, cols]]
        for s in range(SUBLANES):
            cf.append(hf[s:s + 1, :] + pf[s:s + 1, :] * cf[-1])
        cb = [h0_ref[1:2, cols]]
        for s in reversed(range(SUBLANES)):
            cb.append(hb[s:s + 1, :] + pb[s:s + 1, :] * cb[-1])
        st_ref[0:1, cols] = cf[SUBLANES]
        st_ref[1:2, cols] = cb[SUBLANES]
        cins += [_row_select(cf[:SUBLANES]), _row_select(list(reversed(cb[:SUBLANES])))]

    def states(j, carry):
        out = []
        jb = seg - 1 - j
        for lb in range(nb):
            hf = rows(a_f, lb, j) * carry[2 * lb] + rows(b_f, lb, j)
            hb = rows(a_b, lb, jb) * carry[2 * lb + 1] + rows(b_b, lb, jb)
            b_f[lb, pl.ds(j, SUBLANES, stride=pitch), :] = hf
            b_b[lb, pl.ds(jb, SUBLANES, stride=pitch), :] = hb
            out += [hf, hb]
        return tuple(out)

    lax.fori_loop(0, seg, states, tuple(cins))

    for lb in range(nb):
        cols = slice(lb * LANES, (lb + 1) * LANES)
        for s in range(SUBLANES):
            h = b_f[lb, s * pitch:s * pitch + seg, :] + b_b[lb, s * pitch:s * pitch + seg, :]
            gate = _gelu_tanh(g_ref[s * seg:(s + 1) * seg, cols].astype(F32))
            y_ref[s * seg:(s + 1) * seg, cols] = (h * gate).astype(y_ref.dtype)


def lru_mixer(u, conv_w, conv_b, gate_w, gate_b, lam, h0, layer, seq_len, col_block_x, col_block_g):
    m = u.shape[0]
    width = conv_w.shape[-1]
    nb = width // LANES
    n_seq = m // seq_len
    scan_buf = pltpu.VMEM((nb, SUBLANES * (seq_len // SUBLANES + 4), LANES), F32)
    return pl.pallas_call(
        functools.partial(_lru_kernel, seq_len=seq_len),
        out_shape=(jax.ShapeDtypeStruct((m, width), BF16), jax.ShapeDtypeStruct((n_seq, 2, width), F32)),
        grid=(n_seq,),
        in_specs=[
            pl.BlockSpec((seq_len, width), lambda s: (s, col_block_x)),
            pl.BlockSpec((seq_len, width), lambda s: (s, col_block_g)),
            pl.BlockSpec((None, LRU_CONV, width), lambda s: (layer, 0, 0)),
            pl.BlockSpec((None, 1, width), lambda s: (layer, 0, 0)),
            pl.BlockSpec((None, 4, nb, LANES, LANES), lambda s: (layer, 0, 0, 0, 0)),
            pl.BlockSpec((None, 4, width), lambda s: (layer, 0, 0)),
            pl.BlockSpec((None, 2, width), lambda s: (layer, 0, 0)),
            pl.BlockSpec((None, 2, width), lambda s: (s, 0, 0)),
        ],
        out_specs=(
            pl.BlockSpec((seq_len, width), lambda s: (s, 0)),
            pl.BlockSpec((None, 2, width), lambda s: (s, 0, 0)),
        ),
        scratch_shapes=[pltpu.VMEM((seq_len + 2 * SUBLANES, width), F32), scan_buf, scan_buf, scan_buf, scan_buf],
        compiler_params=_params("parallel"),
        name="lru",
    )(u, u, conv_w, conv_b, gate_w, gate_b, lam, h0)


def _head_masks():
    lane = lax.broadcasted_iota(jnp.int32, (1, LANES), 1)
    return lane < LANES // 2, lane >= LANES // 2


def _nt_dot(a, b):
    return lax.dot_general(a, b, (((1,), (1,)), ((), ())), preferred_element_type=F32)


def _scaled(q, scale):
    return (q.astype(F32) * scale).astype(BF16)


def _ctx_attn_kernel(q_ref, k_ref, v_ref, o_ref, *, head_dim):
    masks = _head_masks()
    zero = jnp.zeros((), BF16)
    for p in range(q_ref.shape[-1] // LANES):
        cols = slice(p * LANES, (p + 1) * LANES)
        q = _scaled(q_ref[:, cols], head_dim ** -0.5)
        k = k_ref[:, cols]
        v = v_ref[:, cols]
        out = None
        for msk in masks:
            s = _nt_dot(jnp.where(msk, q, zero), k)
            e = jnp.exp(s - jnp.max(s, axis=-1, keepdims=True))
            den = jnp.sum(e, axis=-1, keepdims=True)
            o = jnp.dot(e.astype(BF16), jnp.where(msk, v, zero), preferred_element_type=F32) / den
            out = o if out is None else out + o
        o_ref[:, cols] = out.astype(o_ref.dtype)


def context_attention(u, seq_len, col_block_q, width, head_dim):
    m = u.shape[0]
    n_seq = m // seq_len
    spec = lambda cb: pl.BlockSpec((seq_len, width), lambda s: (s, cb))
    return pl.pallas_call(
        functools.partial(_ctx_attn_kernel, head_dim=head_dim),
        out_shape=jax.ShapeDtypeStruct((m, width), BF16),
        grid=(n_seq,),
        in_specs=[spec(col_block_q), spec(col_block_q + 1), spec(col_block_q + 2)],
        out_specs=pl.BlockSpec((seq_len, width), lambda s: (s, 0)),
        compiler_params=_params("parallel"),
        name="ctx_attn",
    )(u, u, u)


def _rpb_tile_kernel(rpb_ref, o_ref):
    n_rows, nd = 2 * NA_KH - 1, 2 * NA_KW - 1
    qc = lax.broadcasted_iota(jnp.int32, (GRID_W, 2 * GRID_W), 0)
    lane = lax.broadcasted_iota(jnp.int32, (GRID_W, 2 * GRID_W), 1)
    kc = lane & (GRID_W - 1)
    start = jnp.clip(qc - NA_KW // 2, 0, GRID_W - NA_KW)
    inside = (kc >= start) & (kc < start + NA_KW)
    idx = jnp.clip(kc - qc + NA_KW - 1, 0, nd - 1)
    base = (pl.program_id(0) * pl.num_programs(1) + pl.program_id(1)) * n_rows
    tiles = []
    for r in range(n_rows):
        acc = jnp.zeros((GRID_W, 2 * GRID_W), F32)
        for dd in range(nd):
            acc = jnp.where(idx == dd, rpb_ref[(base + r) * nd + dd], acc)
        tiles.append(jnp.where(inside, acc, NEG_INF))
    for r in range(n_rows - 1):
        o_ref[r] = jnp.where(lane < GRID_W, tiles[r], tiles[r + 1])


def rpb_tiles(rpb):
    depth, heads, nr, _ = rpb.shape
    return pl.pallas_call(
        _rpb_tile_kernel,
        out_shape=jax.ShapeDtypeStruct((depth, heads, nr - 1, GRID_W, 2 * GRID_W), F32),
        grid=(depth, heads),
        in_specs=[pl.BlockSpec(memory_space=pltpu.SMEM)],
        out_specs=pl.BlockSpec((None, None, nr - 1, GRID_W, 2 * GRID_W), lambda l, h: (l, h, 0, 0, 0)),
        compiler_params=_params("parallel", "parallel"),
        name="rpb_tiles",
    )(rpb.reshape(-1))


NBR_ROWS = 4


def _nbr_attn_kernel(q_ref, k_ref, v_ref, ck_ref, cv_ref, tile_ref, o_ref, *, head_dim, rows, kh):
    masks = _head_masks()
    zero = jnp.zeros((), BF16)
    w = GRID_W
    windows, deltas = [], []
    for rr in range(NBR_ROWS):
        r = pl.program_id(1) * NBR_ROWS + rr
        r0 = jnp.clip(r - kh // 2, 0, rows - kh)
        windows.append(pl.ds(pl.multiple_of(r0 * w, w), kh * w))
        deltas.append(r - r0)
    for p in range(q_ref.shape[-1] // LANES):
        cols = slice(p * LANES, (p + 1) * LANES)
        q = _scaled(q_ref[:, cols], head_dim ** -0.5)
        ck = ck_ref[:, cols]
        cv = cv_ref[:, cols]
        kw = [k_ref[win, cols] for win in windows]
        vw = [v_ref[win, cols] for win in windows]
        out = None
        for hh, msk in enumerate(masks):
            qm = jnp.where(msk, q, zero)
            s_cx = _nt_dot(qm, ck)
            s_nb = []
            for rr in range(NBR_ROWS):
                first = NA_KH - 1 - deltas[rr]
                bias = jnp.concatenate([tile_ref[2 * p + hh, first + 2 * jj] for jj in range(kh // 2)], axis=-1)
                s_nb.append(_nt_dot(qm[rr * w:(rr + 1) * w, :], kw[rr]) + bias)
            s_nb = jnp.concatenate(s_nb, axis=0)
            mx = jnp.maximum(jnp.max(s_nb, axis=-1, keepdims=True), jnp.max(s_cx, axis=-1, keepdims=True))
            e_nb = jnp.exp(s_nb - mx)
            e_cx = jnp.exp(s_cx - mx)
            den = jnp.sum(e_nb, axis=-1, keepdims=True) + jnp.sum(e_cx, axis=-1, keepdims=True)
            e_nb = e_nb.astype(BF16)
            o_nb = jnp.concatenate(
                [jnp.dot(e_nb[rr * w:(rr + 1) * w, :], jnp.where(msk, vw[rr], zero), preferred_element_type=F32)
                 for rr in range(NBR_ROWS)], axis=0)
            o = jnp.dot(e_cx.astype(BF16), jnp.where(msk, cv, zero), preferred_element_type=F32)
            o = (o + o_nb) / den
            out = o if out is None else out + o
        o_ref[:, cols] = out.astype(o_ref.dtype)


def neighbourhood_attention(u, ctx_k, ctx_v, tiles, layer, seq_len, col_block_q, width, head_dim):
    m = u.shape[0]
    n_seq = m // seq_len
    rows = seq_len // GRID_W
    kh = min(NA_KH, rows)
    assert kh % 2 == 0 and rows % NBR_ROWS == 0
    groups = rows // NBR_ROWS
    past = ctx_k.shape[2]
    return pl.pallas_call(
        functools.partial(_nbr_attn_kernel, head_dim=head_dim, rows=rows, kh=kh),
        out_shape=jax.ShapeDtypeStruct((m, width), BF16),
        grid=(n_seq, groups),
        in_specs=[
            pl.BlockSpec((NBR_ROWS * GRID_W, width), lambda b, g: (b * groups + g, col_block_q)),
            pl.BlockSpec((seq_len, width), lambda b, g: (b, col_block_q + 1)),
            pl.BlockSpec((seq_len, width), lambda b, g: (b, col_block_q + 2)),
            pl.BlockSpec((None, None, past, width), lambda b, g: (b, layer, 0, 0)),
            pl.BlockSpec((None, None, past, width), lambda b, g: (b, layer, 0, 0)),
            pl.BlockSpec((None,) + tiles.shape[1:], lambda b, g: (layer, 0, 0, 0, 0)),
        ],
        out_specs=pl.BlockSpec((NBR_ROWS * GRID_W, width), lambda b, g: (b * groups + g, 0)),
        compiler_params=_params("parallel", "parallel"),
        name="nbr_attn",
    )(u, u, u, ctx_k, ctx_v, tiles)


def _out_proj_kernel(y0_ref, y1_ref, y2_ref, y3_ref, w_ref, x_ref, mod_ref, g_ref, o_ref, *, d, gw):
    gg = g_ref[...] * mod_ref[:, 2 * d:3 * d]
    for r in range(x_ref.shape[0] // ROW_CHUNK):
        rows = slice(r * ROW_CHUNK, (r + 1) * ROW_CHUNK)
        mix = None
        for gi, y_ref in enumerate((y0_ref, y1_ref, y2_ref, y3_ref)):
            part = jnp.dot(y_ref[rows, :], w_ref[gi * gw:(gi + 1) * gw, :], preferred_element_type=F32)
            mix = part if mix is None else mix + part
        rs = lax.rsqrt(jnp.mean(mix * mix, axis=-1, keepdims=True) + EPS)
        o_ref[rows, :] = x_ref[rows, :] + mix * rs * gg


def out_proj(ys, w_out, x, mods, g_post, layer, seq_len, cond_base, cond_stride, tm=512):
    m, d = x.shape
    gw = ys[0].shape[-1]
    tm = min(tm, m)
    cidx = functools.partial(_cond_index, tm=tm, seq_len=seq_len, cond_base=cond_base, cond_stride=cond_stride)
    yspec = pl.BlockSpec((tm, gw), lambda i: (i, 0))
    return pl.pallas_call(
        functools.partial(_out_proj_kernel, d=d, gw=gw),
        out_shape=jax.ShapeDtypeStruct((m, d), F32),
        grid=(m // tm,),
        in_specs=[
            yspec, yspec, yspec, yspec,
            pl.BlockSpec((None, w_out.shape[1], d), lambda i: (layer, 0, 0)),
            pl.BlockSpec((tm, d), lambda i: (i, 0)),
            pl.BlockSpec((None, None, 1, N_MOD * d), lambda i: (layer, cidx(i), 0, 0)),
            pl.BlockSpec((None, 1, d), lambda i: (layer, 0, 0)),
        ],
        out_specs=pl.BlockSpec((tm, d), lambda i: (i, 0)),
        compiler_params=_params("parallel"),
        name="out_proj",
    )(*ys, w_out, x, mods, g_post)


def _ffn_kernel(x_ref, mod_ref, gpre_ref, gpost_ref, w1_ref, w3_ref, w2_ref, o_ref, h_ref, *, d, rb):
    j = pl.program_id(1)

    @pl.when(j == 0)
    def _():
        _norm_modulate(x_ref, gpre_ref[...], mod_ref[:, 3 * d:4 * d], mod_ref[:, 4 * d:5 * d], h_ref)

    def step(first):
        w1, w3, w2 = (w[...].astype(BF16) for w in (w1_ref, w3_ref, w2_ref))
        for r in range(h_ref.shape[0] // rb):
            rows = slice(r * rb, (r + 1) * rb)
            h = h_ref[rows, :]
            a = jnp.dot(h, w1, preferred_element_type=F32)
            b = jnp.dot(h, w3, preferred_element_type=F32)
            hid = (a * _sigmoid(a) * b).astype(BF16)
            part = jnp.dot(hid, w2, preferred_element_type=F32)
            if first:
                o_ref[rows, :] = part
            else:
                o_ref[rows, :] += part

    last = pl.num_programs(1) - 1

    @pl.when(j == 0)
    def _():
        step(True)

    @pl.when((j > 0) & (j < last))
    def _():
        step(False)

    @pl.when(j == last)
    def _():
        step(False)
        _gated_residual(x_ref, o_ref, gpost_ref[...], mod_ref[:, 5 * d:6 * d], o_ref)


def ffn(x, mods, g_pre, g_post, w1, w3, w2, layer, seq_len, cond_base, cond_stride, tm=1024, tf=256, rb=512):
    m, d = x.shape
    f = w1.shape[-1]
    tm = min(tm, m)
    rb = min(rb, tm)
    cidx = functools.partial(_cond_index, tm=tm, seq_len=seq_len, cond_base=cond_base, cond_stride=cond_stride)
    return pl.pallas_call(
        functools.partial(_ffn_kernel, d=d, rb=rb),
        out_shape=jax.ShapeDtypeStruct((m, d), F32),
        grid=(m // tm, f // tf),
        in_specs=[
            pl.BlockSpec((tm, d), lambda i, j: (i, 0)),
            pl.BlockSpec((None, None, 1, N_MOD * d), lambda i, j: (layer, cidx(i), 0, 0)),
            pl.BlockSpec((None, 1, d), lambda i, j: (layer, 0, 0)),
            pl.BlockSpec((None, 1, d), lambda i, j: (layer, 0, 0)),
            pl.BlockSpec((None, d, tf), lambda i, j: (layer, 0, j)),
            pl.BlockSpec((None, d, tf), lambda i, j: (layer, 0, j)),
            pl.BlockSpec((None, tf, d), lambda i, j: (layer, j, 0)),
        ],
        out_specs=pl.BlockSpec((tm, d), lambda i, j: (i, 0)),
        scratch_shapes=[pltpu.VMEM((tm, d), BF16)],
        compiler_params=_params("parallel", "arbitrary"),
        name="ffn",
    )(x, mods, g_pre, g_post, w1, w3, w2)


def _block_diag_gates(gate_w):
    depth, _, _, nblk, c, _ = gate_w.shape
    per = LANES // c
    w = gate_w.reshape(depth, 4, nblk // per, per, c, c).astype(BF16)
    eye = jnp.eye(per, dtype=BF16)
    tiles = jnp.einsum('lgnpce,pq->lgnpcqe', w, eye)
    return tiles.reshape(depth, 4, nblk // per, LANES, LANES)


def kernel(x_prompt, x_sample, cache_k, cache_v, state_lru, c, c_ctx, w_ada, b_ada, g_pre_mix, g_post_mix,
           g_pre_ffn, g_post_ffn, w_in, pool_w, pool_scale, lru_conv_w, lru_conv_b, lru_gate_w, lru_gate_b,
           lru_lambda, cm_dw_w, cm_dw_b, cm_ln_g, cm_ln_b, cm_pw_w, cm_pw_b, na_rpb, w_out, ffn_w1, ffn_w3, ffn_w2):
    batch, seq, d = x_prompt.shape
    dec_batch, dec_seq, _ = x_sample.shape
    depth = w_in.shape[0]
    gw = pool_scale.shape[-1]
    heads, head_dim = cache_k.shape[-2], cache_k.shape[-1]
    past = cache_k.shape[2]
    assert dec_batch + 1 <= 8 and seq % SEQ_CHUNK == 0 and dec_seq % SEQ_CHUNK == 0

    cond = jnp.zeros((8, d), F32).at[0].set(c_ctx).at[1:1 + dec_batch].set(c)
    mods = adaln_all(cond, w_ada, b_ada).reshape(depth, 8, 1, N_MOD * d)

    vec = lambda a: a.reshape(depth, 1, a.shape[-1])
    g_pre_mix, g_post_mix, g_pre_ffn, g_post_ffn = map(vec, (g_pre_mix, g_post_mix, g_pre_ffn, g_post_ffn))
    pool_scale, lru_conv_b, cm_dw_b, cm_ln_g, cm_ln_b, cm_pw_b = map(
        vec, (pool_scale, lru_conv_b, cm_dw_b, cm_ln_g, cm_ln_b, cm_pw_b))
    w_in_b, w_out_b = w_in.astype(BF16), w_out.astype(BF16)
    pool_w_b, cm_pw_b16 = pool_w.astype(BF16), cm_pw_w.astype(BF16)
    gate_tiles = _block_diag_gates(lru_gate_w)
    gate_b = lru_gate_b.reshape(depth, 4, gw)
    tiles = rpb_tiles(na_rpb)
    ctx_k = cache_k.reshape(dec_batch, depth, past, gw).astype(BF16)
    ctx_v = cache_v.reshape(dec_batch, depth, past, gw).astype(BF16)

    cb_pool, cb_lru_x, cb_lru_g, cb_cm, cb_q = 0, 1, 2, 3, 5

    def layer_step(x, l, seq_len, cond_base, cond_stride, h0, kv):
        res = in_proj(x, mods, g_pre_mix, w_in_b, l, seq_len, cond_base, cond_stride, kv=kv)
        u = res[0]
        y_pool, y_cm = pool_conformer(u, pool_w_b, pool_scale, cm_dw_w, cm_dw_b, cm_ln_g, cm_ln_b, cm_pw_b16, cm_pw_b,
                                      l, seq_len, cb_pool, cb_cm)
        y_lru, st = lru_mixer(u, lru_conv_w, lru_conv_b, gate_tiles, gate_b, lru_lambda, h0, l, seq_len,
                              cb_lru_x, cb_lru_g)
        if kv is None:
            y_att = neighbourhood_attention(u, ctx_k, ctx_v, tiles, l, seq_len, cb_q, gw, head_dim)
        else:
            y_att = context_attention(u, seq_len, cb_q, gw, head_dim)
        x = out_proj((y_pool, y_lru, y_cm, y_att), w_out_b, x, mods, g_post_mix, l, seq_len, cond_base, cond_stride)
        x = ffn(x, mods, g_pre_ffn, g_post_ffn, ffn_w1, ffn_w3, ffn_w2, l, seq_len, cond_base, cond_stride)
        return x, st, tuple(res[1:])

    xp = x_prompt.reshape(batch * seq, d)
    xs = x_sample.reshape(dec_batch * dec_seq, d)
    zeros_h0 = jnp.zeros((batch, 2, gw), F32)
    kv = tuple(jnp.zeros((batch, depth, seq, gw), F32) for _ in range(2))
    sts = []
    for l in range(depth):
        xp, st, kv = layer_step(xp, l, seq, 0, 0, zeros_h0, kv)
        sts.append(st)
        xs, _, _ = layer_step(xs, l, dec_seq, 1, 1, state_lru[:, l], None)

    out_kv = (batch, depth, seq, heads, head_dim)
    return (xp.reshape(batch, seq, d), xs.reshape(dec_batch, dec_seq, d),
            kv[0].reshape(out_kv), kv[1].reshape(out_kv), jnp.stack(sts, axis=1))
```

```python
import functools

import jax
import jax.numpy as jnp
from jax import lax
from jax.experimental import pallas as pl
from jax.experimental.pallas import tpu as pltpu

F32 = jnp.float32
BF16 = jnp.bfloat16

EPS = 1e-6
NEG_INF = -1e30
N_MOD = 6
POOL_WINDOWS = (2, 4, 8, 16)
LRU_CONV = 4
LRU_C = 8.0
CM_CONV = 31
NA_KH = 8
NA_KW = 16
GRID_W = 64

LANES = 128
SUBLANES = 8
VMEM_LIMIT_BYTES = 60 * 1024 * 1024
SEQ_CHUNK = 256
HALO = 16


def _params(*sem):
    return pltpu.CompilerParams(dimension_semantics=sem, vmem_limit_bytes=VMEM_LIMIT_BYTES)


def _sigmoid(x):
    return 0.5 * jnp.tanh(0.5 * x) + 0.5


def _adaln_kernel(cond_ref, w_ref, b_ref, o_ref):
    c = cond_ref[...]
    s = (c * _sigmoid(c)).astype(BF16)
    o_ref[...] = jnp.dot(s, w_ref[...].astype(BF16), preferred_element_type=F32) + b_ref[...]


def adaln_all(cond, w_ada, b_ada, tn=1024):
    depth, d, n = w_ada.shape
    return pl.pallas_call(
        _adaln_kernel,
        out_shape=jax.ShapeDtypeStruct((depth, 8, n), F32),
        grid=(depth, n // tn),
        in_specs=[
            pl.BlockSpec((8, d), lambda l, j: (0, 0)),
            pl.BlockSpec((None, d, tn), lambda l, j: (l, 0, j)),
            pl.BlockSpec((None, 1, tn), lambda l, j: (l, 0, j)),
        ],
        out_specs=pl.BlockSpec((None, 8, tn), lambda l, j: (l, 0, j)),
        compiler_params=_params("parallel", "parallel"),
        name="adaln",
    )(cond, w_ada, b_ada.reshape(depth, 1, n))


ROW_CHUNK = 128


def _norm_modulate(x_ref, g, shift, scale, h_ref):
    gs = g * (1.0 + scale)
    for r in range(x_ref.shape[0] // ROW_CHUNK):
        rows = slice(r * ROW_CHUNK, (r + 1) * ROW_CHUNK)
        x = x_ref[rows, :]
        rs = lax.rsqrt(jnp.mean(x * x, axis=-1, keepdims=True) + EPS)
        h_ref[rows, :] = (x * rs * gs + shift).astype(h_ref.dtype)


def _gated_residual(x_ref, y_ref, g, gate, o_ref):
    gg = g * gate
    for r in range(x_ref.shape[0] // ROW_CHUNK):
        rows = slice(r * ROW_CHUNK, (r + 1) * ROW_CHUNK)
        y = y_ref[rows, :]
        rs = lax.rsqrt(jnp.mean(y * y, axis=-1, keepdims=True) + EPS)
        o_ref[rows, :] = x_ref[rows, :] + y * rs * gg


def _in_proj_kernel(x_ref, mod_ref, g_ref, w_ref, *rest, d, tn, with_kv):
    if with_kv:
        rest = rest[2:]
        k_ref, v_ref = rest[-3:-1]
    h_ref = rest[-1]
    n = w_ref.shape[-1]
    half = tn // 2
    u_refs = rest[:n // half]
    _norm_modulate(x_ref, g_ref[...], mod_ref[:, 0:d], mod_ref[:, d:2 * d], h_ref)
    for c in range(n // tn):
        res = jnp.dot(h_ref[...], w_ref[:, c * tn:(c + 1) * tn], preferred_element_type=F32)
        u_refs[2 * c][...] = res[:, :half].astype(BF16)
        u_refs[2 * c + 1][...] = res[:, half:].astype(BF16)
        if with_kv and c == n // tn - 1:
            k_ref[...] = res[:, :half].reshape(k_ref.shape)
            v_ref[...] = res[:, half:].reshape(v_ref.shape)


def _cond_index(i, tm, seq_len, cond_base, cond_stride):
    assert cond_stride == 0 or seq_len % tm == 0, "a row tile must not straddle two conditioning vectors"
    return cond_base + ((i * tm) // seq_len) * cond_stride


def in_proj(x, mods, g_pre, w_in, layer, seq_len, cond_base, cond_stride, kv=None, tm=512, tn=1024):
    m, d = x.shape
    n = w_in.shape[-1]
    tm = min(tm, m)
    assert n % tn == 0
    cidx = functools.partial(_cond_index, tm=tm, seq_len=seq_len, cond_base=cond_base, cond_stride=cond_stride)
    in_specs = [
        pl.BlockSpec((tm, d), lambda i: (i, 0)),
        pl.BlockSpec((None, None, 1, N_MOD * d), lambda i: (layer, cidx(i), 0, 0)),
        pl.BlockSpec((None, 1, d), lambda i: (layer, 0, 0)),
        pl.BlockSpec((None, d, n), lambda i: (layer, 0, 0), pipeline_mode=pl.Buffered(1)),
    ]
    n_out = 2 * n // tn
    out_shape = [jax.ShapeDtypeStruct((m, tn // 2), BF16)] * n_out
    out_specs = [pl.BlockSpec((tm, tn // 2), lambda i: (i, 0))] * n_out
    args = [x, mods, g_pre, w_in]
    aliases = {}
    if kv is not None:
        width = kv[0].shape[-1]
        assert tn == 2 * width and tm % seq_len == 0
        kv_spec = pl.BlockSpec((tm // seq_len, None, seq_len, width), lambda i: (i, layer, 0, 0))
        in_specs += [pl.BlockSpec(memory_space=pl.ANY)] * 2
        out_shape += [jax.ShapeDtypeStruct(kv[0].shape, F32)] * 2
        out_specs += [kv_spec, kv_spec]
        args += list(kv)
        aliases = {4: n_out, 5: n_out + 1}
    return pl.pallas_call(
        functools.partial(_in_proj_kernel, d=d, tn=tn, with_kv=kv is not None),
        out_shape=out_shape,
        grid=(m // tm,),
        in_specs=in_specs,
        out_specs=out_specs,
        scratch_shapes=[pltpu.VMEM((tm, d), BF16)],
        input_output_aliases=aliases,
        compiler_params=_params("parallel"),
        name="in_proj",
    )(*args)


def _halo_maps(seq_len, n_rows):
    chunks = seq_len // SEQ_CHUNK
    per = SEQ_CHUNK // HALO
    nblk = n_rows // HALO

    def prev_map(s, c):
        return (jnp.maximum((s * chunks + c) * per - 1, 0), 0)

    def next_map(s, c):
        return (jnp.minimum((s * chunks + c + 1) * per, nblk - 1), 0)

    return prev_map, next_map


def _pool_body(prev_ref, cur_ref, next_ref, w_ref, sc_ref, o_ref, pad_ref, *, chunks, seq_len):
    c = pl.program_id(1)
    cur = cur_ref[...].astype(F32)
    pad_ref[0:HALO, :] = jnp.where(c == 0, 0.0, prev_ref[...].astype(F32))
    pad_ref[HALO:HALO + SEQ_CHUNK, :] = cur
    pad_ref[HALO + SEQ_CHUNK:2 * HALO + SEQ_CHUNK, :] = jnp.where(c == chunks - 1, 0.0, next_ref[...].astype(F32))
    t = c * SEQ_CHUNK + lax.broadcasted_iota(jnp.int32, (SEQ_CHUNK, LANES), 0)
    for gi, w in enumerate(POOL_WINDOWS):
        cols = slice(gi * LANES, (gi + 1) * LANES)
        acc = pad_ref[HALO - w // 2:HALO - w // 2 + SEQ_CHUNK, cols]
        for off in range(-(w // 2) + 1, w - w // 2):
            acc = acc + pad_ref[HALO + off:HALO + off + SEQ_CHUNK, cols]
        lo = jnp.clip(t - w // 2, 0, seq_len)
        hi = jnp.clip(t - w // 2 + w, 0, seq_len)
        pooled = acc / (hi - lo).astype(F32) - cur[:, cols]
        y = jnp.dot(pooled.astype(BF16), w_ref[gi], preferred_element_type=F32) * sc_ref[:, cols]
        o_ref[:, cols] = y.astype(o_ref.dtype)


def _conformer_body(ap_ref, a_ref, an_ref, gp_ref, g_ref, gn_ref, dw_ref, dwb_ref, lng_ref, lnb_ref,
                    pw_ref, pwb_ref, o_ref, pad_ref, sh_ref, *, chunks):
    c = pl.program_id(1)
    glu = lambda a, g: a[...].astype(F32) * _sigmoid(g[...].astype(F32))
    pad_ref[0:HALO, :] = jnp.where(c == 0, 0.0, glu(ap_ref, gp_ref))
    pad_ref[HALO:HALO + SEQ_CHUNK, :] = glu(a_ref, g_ref)
    pad_ref[HALO + SEQ_CHUNK:2 * HALO + SEQ_CHUNK, :] = jnp.where(c == chunks - 1, 0.0, glu(an_ref, gn_ref))
    n_sh = sh_ref.shape[1]
    for r in range(1, SUBLANES):
        sh_ref[r - 1] = pad_ref[r:r + n_sh, :]
    first = HALO - CM_CONV // 2
    acc = None
    for k in range(CM_CONV):
        q, r = divmod(first + k, SUBLANES)
        rows = slice(q * SUBLANES, q * SUBLANES + SEQ_CHUNK)
        tap = dw_ref[k:k + 1, :] * (pad_ref[rows, :] if r == 0 else sh_ref[r - 1, rows, :])
        acc = tap + dwb_ref[...] if acc is None else acc + tap
    mu = jnp.mean(acc, axis=-1, keepdims=True)
    cen = acc - mu
    var = jnp.mean(cen * cen, axis=-1, keepdims=True)
    y = cen * lax.rsqrt(var + EPS) * lng_ref[...] + lnb_ref[...]
    z = (y * _sigmoid(y)).astype(BF16)
    o = jnp.dot(z, pw_ref[...], preferred_element_type=F32) + pwb_ref[...]
    o_ref[...] = o.astype(o_ref.dtype)


def _pool_conformer_kernel(*refs, chunks, seq_len):
    pool_in, cm_in = refs[0:5], refs[5:17]
    pool_out, cm_out, pool_pad, cm_pad, cm_sh = refs[17:22]
    _pool_body(*pool_in, pool_out, pool_pad, chunks=chunks, seq_len=seq_len)
    _conformer_body(*cm_in, cm_out, cm_pad, cm_sh, chunks=chunks)


def pool_conformer(u_pool, u_a, u_g, pool_w, pool_scale, dw_w, dw_b, ln_g, ln_b, pw_w, pw_b, layer, seq_len):
    m = u_pool.shape[0]
    width = dw_w.shape[-1]
    assert width == len(POOL_WINDOWS) * LANES
    n_seq = m // seq_len
    chunks = seq_len // SEQ_CHUNK
    padded = SEQ_CHUNK + 2 * HALO
    vec = lambda: pl.BlockSpec((None, 1, width), lambda s, c: (layer, 0, 0))

    prev_map, next_map = _halo_maps(seq_len, m)
    chunk_specs = [pl.BlockSpec((HALO, width), prev_map),
                   pl.BlockSpec((SEQ_CHUNK, width), lambda s, c: (s * chunks + c, 0)),
                   pl.BlockSpec((HALO, width), next_map)]
    out_spec = pl.BlockSpec((SEQ_CHUNK, width), lambda s, c: (s * chunks + c, 0))
    return pl.pallas_call(
        functools.partial(_pool_conformer_kernel, chunks=chunks, seq_len=seq_len),
        out_shape=(jax.ShapeDtypeStruct((m, width), BF16),) * 2,
        grid=(n_seq, chunks),
        in_specs=chunk_specs + [
            pl.BlockSpec((None, len(POOL_WINDOWS), LANES, LANES), lambda s, c: (layer, 0, 0, 0)),
            vec(),
        ] + chunk_specs + chunk_specs + [
            pl.BlockSpec((None, CM_CONV, width), lambda s, c: (layer, 0, 0)),
            vec(), vec(), vec(),
            pl.BlockSpec((None, width, width), lambda s, c: (layer, 0, 0)),
            vec(),
        ],
        out_specs=(out_spec, out_spec),
        scratch_shapes=[pltpu.VMEM((padded, width), F32), pltpu.VMEM((padded, width), F32),
                        pltpu.VMEM((SUBLANES - 1, padded - SUBLANES, width), F32)],
        compiler_params=_params("parallel", "parallel"),
        name="pool_conformer",
    )(u_pool, u_pool, u_pool, pool_w, pool_scale, u_a, u_a, u_a, u_g, u_g, u_g, dw_w, dw_b, ln_g, ln_b, pw_w, pw_b)


def _softplus(z):
    return jnp.maximum(z, 0.0) + jnp.log1p(jnp.exp(-jnp.abs(z)))


def _gelu_tanh(x):
    return 0.5 * x * (1.0 + jnp.tanh(0.7978845608028654 * (x + 0.044715 * (x * x * x))))


def _row_select(rows):
    sub = lax.broadcasted_iota(jnp.int32, (SUBLANES, LANES), 0)
    out = jnp.zeros((SUBLANES, LANES), F32)
    for s, r in enumerate(rows):
        out = jnp.where(sub == s, r, out)
    return out


def _lru_kernel(x_ref, g_ref, cw_ref, cb_ref, gw_ref, gb_ref, lam_ref, h0_ref, y_ref, st_ref,
                xpad, a_f, b_f, a_b, b_b, *, seq_len):
    t_len = seq_len
    nb = x_ref.shape[-1] // LANES
    seg = t_len // SUBLANES
    pitch = seg + 4
    width = x_ref.shape[-1]
    xpad[0:SUBLANES, :] = jnp.zeros((SUBLANES, width), F32)
    xpad[SUBLANES + t_len:2 * SUBLANES + t_len, :] = jnp.zeros((SUBLANES, width), F32)
    xpad[SUBLANES:SUBLANES + t_len, :] = x_ref[...].astype(F32)
    left = LRU_CONV // 2
    sp = _softplus(-lam_ref[...])
    for lb in range(nb):
        cols = slice(lb * LANES, (lb + 1) * LANES)
        x = cb_ref[:, cols] + cw_ref[0:1, cols] * xpad[SUBLANES - left:SUBLANES - left + t_len, cols]
        for k in range(1, LRU_CONV):
            x = x + cw_ref[k:k + 1, cols] * xpad[SUBLANES - left + k:SUBLANES - left + k + t_len, cols]
        xb = x.astype(BF16)
        for d, (a_ref, b_ref) in enumerate(((a_f, b_f), (a_b, b_b))):
            gate = lambda gi: _sigmoid(jnp.dot(xb, gw_ref[gi, lb], preferred_element_type=F32) + gb_ref[gi:gi + 1, cols])
            r, i = gate(2 * d), gate(2 * d + 1)
            log_a = (-LRU_C) * r * sp[d:d + 1, cols]
            a = jnp.exp(log_a)
            bt = jnp.sqrt(jnp.maximum(-jnp.tanh(log_a) * (a * a + 1.0), 0.0)) * (i * x)
            for s in range(SUBLANES):
                a_ref[lb, s * pitch:s * pitch + seg, :] = a[s * seg:(s + 1) * seg, :]
                b_ref[lb, s * pitch:s * pitch + seg, :] = bt[s * seg:(s + 1) * seg, :]

    def rows(ref, lb, j):
        return ref[lb, pl.ds(j, SUBLANES, stride=pitch), :]

    zeros = jnp.zeros((SUBLANES, LANES), F32)
    ones = jnp.ones((SUBLANES, LANES), F32)

    def totals(j, carry):
        out = []
        jb = seg - 1 - j
        for lb in range(nb):
            hf, pf, hb, pb = carry[4 * lb:4 * lb + 4]
            af, ab = rows(a_f, lb, j), rows(a_b, lb, jb)
            out += [af * hf + rows(b_f, lb, j), af * pf, ab * hb + rows(b_b, lb, jb), ab * pb]
        return tuple(out)

    tot = lax.fori_loop(0, seg, totals, (zeros, ones, zeros, ones) * nb)

    cins = []
    for lb in range(nb):
        cols = slice(lb * LANES, (lb + 1) * LANES)
        hf, pf, hb, pb = tot[4 * lb:4 * lb + 4]
        cf = [h0_ref[0:1, cols]]
        for s in range(SUBLANES):
            cf.append(hf[s:s + 1, :] + pf[s:s + 1, :] * cf[-1])
        cb = [h0_ref[1:2, cols]]
        for s in reversed(range(SUBLANES)):
            cb.append(hb[s:s + 1, :] + pb[s:s + 1, :] * cb[-1])
        st_ref[0:1, cols] = cf[SUBLANES]
        st_ref[1:2, cols] = cb[SUBLANES]
        cins += [_row_select(cf[:SUBLANES]), _row_select(list(reversed(cb[:SUBLANES])))]

    def states(j, carry):
        out = []
        jb = seg - 1 - j
        for lb in range(nb):
            hf = rows(a_f, lb, j) * carry[2 * lb] + rows(b_f, lb, j)
            hb = rows(a_b, lb, jb) * carry[2 * lb + 1] + rows(b_b, lb, jb)
            b_f[lb, pl.ds(j, SUBLANES, stride=pitch), :] = hf
            b_b[lb, pl.ds(jb, SUBLANES, stride=pitch), :] = hb
            out += [hf, hb]
        return tuple(out)

    lax.fori_loop(0, seg, states, tuple(cins))

    for lb in range(nb):
        cols = slice(lb * LANES, (lb + 1) * LANES)
        for s in range(SUBLANES):
            h = b_f[lb, s * pitch:s * pitch + seg, :] + b_b[lb, s * pitch:s * pitch + seg, :]
            gate = _gelu_tanh(g_ref[s * seg:(s + 1) * seg, cols].astype(F32))
            y_ref[s * seg:(s + 1) * seg, cols] = (h * gate).astype(y_ref.dtype)


def lru_mixer(u_x, u_g, conv_w, conv_b, gate_w, gate_b, lam, h0, layer, seq_len):
    m = u_x.shape[0]
    width = conv_w.shape[-1]
    nb = width // LANES
    n_seq = m // seq_len
    scan_buf = pltpu.VMEM((nb, SUBLANES * (seq_len // SUBLANES + 4), LANES), F32)
    return pl.pallas_call(
        functools.partial(_lru_kernel, seq_len=seq_len),
        out_shape=(jax.ShapeDtypeStruct((m, width), BF16), jax.ShapeDtypeStruct((n_seq, 2, width), F32)),
        grid=(n_seq,),
        in_specs=[
            pl.BlockSpec((seq_len, width), lambda s: (s, 0)),
            pl.BlockSpec((seq_len, width), lambda s: (s, 0)),
            pl.BlockSpec((None, LRU_CONV, width), lambda s: (layer, 0, 0)),
            pl.BlockSpec((None, 1, width), lambda s: (layer, 0, 0)),
            pl.BlockSpec((None, 4, nb, LANES, LANES), lambda s: (layer, 0, 0, 0, 0)),
            pl.BlockSpec((None, 4, width), lambda s: (layer, 0, 0)),
            pl.BlockSpec((None, 2, width), lambda s: (layer, 0, 0)),
            pl.BlockSpec((None, 2, width), lambda s: (s, 0, 0)),
        ],
        out_specs=(
            pl.BlockSpec((seq_len, width), lambda s: (s, 0)),
            pl.BlockSpec((None, 2, width), lambda s: (s, 0, 0)),
        ),
        scratch_shapes=[pltpu.VMEM((seq_len + 2 * SUBLANES, width), F32), scan_buf, scan_buf, scan_buf, scan_buf],
        compiler_params=_params("parallel"),
        name="lru",
    )(u_x, u_g, conv_w, conv_b, gate_w, gate_b, lam, h0)


def _head_masks():
    lane = lax.broadcasted_iota(jnp.int32, (1, LANES), 1)
    return lane < LANES // 2, lane >= LANES // 2


def _nt_dot(a, b):
    return lax.dot_general(a, b, (((1,), (1,)), ((), ())), preferred_element_type=F32)


def _scaled(q, scale):
    return (q.astype(F32) * scale).astype(BF16)


def _ctx_attn_kernel(q_ref, k_ref, v_ref, o_ref, *, head_dim):
    masks = _head_masks()
    zero = jnp.zeros((), BF16)
    for p in range(q_ref.shape[-1] // LANES):
        cols = slice(p * LANES, (p + 1) * LANES)
        q = _scaled(q_ref[:, cols], head_dim ** -0.5)
        k = k_ref[:, cols]
        v = v_ref[:, cols]
        out = None
        for msk in masks:
            s = _nt_dot(jnp.where(msk, q, zero), k)
            e = jnp.exp(s - jnp.max(s, axis=-1, keepdims=True))
            den = jnp.sum(e, axis=-1, keepdims=True)
            o = jnp.dot(e.astype(BF16), jnp.where(msk, v, zero), preferred_element_type=F32) / den
            out = o if out is None else out + o
        o_ref[:, cols] = out.astype(o_ref.dtype)


def context_attention(q, k, v, seq_len, head_dim):
    m, width = q.shape
    n_seq = m // seq_len
    spec = pl.BlockSpec((seq_len, width), lambda s: (s, 0))
    return pl.pallas_call(
        functools.partial(_ctx_attn_kernel, head_dim=head_dim),
        out_shape=jax.ShapeDtypeStruct((m, width), BF16),
        grid=(n_seq,),
        in_specs=[spec, spec, spec],
        out_specs=spec,
        compiler_params=_params("parallel"),
        name="ctx_attn",
    )(q, k, v)


def _rpb_tile_kernel(rpb_ref, o_ref):
    n_rows, nd = 2 * NA_KH - 1, 2 * NA_KW - 1
    qc = lax.broadcasted_iota(jnp.int32, (GRID_W, 2 * GRID_W), 0)
    lane = lax.broadcasted_iota(jnp.int32, (GRID_W, 2 * GRID_W), 1)
    kc = lane & (GRID_W - 1)
    start = jnp.clip(qc - NA_KW // 2, 0, GRID_W - NA_KW)
    inside = (kc >= start) & (kc < start + NA_KW)
    idx = jnp.clip(kc - qc + NA_KW - 1, 0, nd - 1)
    base = (pl.program_id(0) * pl.num_programs(1) + pl.program_id(1)) * n_rows
    tiles = []
    for r in range(n_rows):
        acc = jnp.zeros((GRID_W, 2 * GRID_W), F32)
        for dd in range(nd):
            acc = jnp.where(idx == dd, rpb_ref[(base + r) * nd + dd], acc)
        tiles.append(jnp.where(inside, acc, NEG_INF))
    for r in range(n_rows - 1):
        o_ref[r] = jnp.where(lane < GRID_W, tiles[r], tiles[r + 1])


def rpb_tiles(rpb):
    depth, heads, nr, _ = rpb.shape
    return pl.pallas_call(
        _rpb_tile_kernel,
        out_shape=jax.ShapeDtypeStruct((depth, heads, nr - 1, GRID_W, 2 * GRID_W), F32),
        grid=(depth, heads),
        in_specs=[pl.BlockSpec(memory_space=pltpu.SMEM)],
        out_specs=pl.BlockSpec((None, None, nr - 1, GRID_W, 2 * GRID_W), lambda l, h: (l, h, 0, 0, 0)),
        compiler_params=_params("parallel", "parallel"),
        name="rpb_tiles",
    )(rpb.reshape(-1))


NBR_ROWS = 4


def _nbr_attn_kernel(q_ref, k_ref, v_ref, ck_ref, cv_ref, tile_ref, o_ref, *, head_dim, rows, kh):
    masks = _head_masks()
    zero = jnp.zeros((), BF16)
    w = GRID_W
    windows, deltas = [], []
    for rr in range(NBR_ROWS):
        r = pl.program_id(1) * NBR_ROWS + rr
        r0 = jnp.clip(r - kh // 2, 0, rows - kh)
        windows.append(pl.ds(pl.multiple_of(r0 * w, w), kh * w))
        deltas.append(r - r0)
    for p in range(q_ref.shape[-1] // LANES):
        cols = slice(p * LANES, (p + 1) * LANES)
        q = _scaled(q_ref[:, cols], head_dim ** -0.5)
        ck = ck_ref[:, cols]
        cv = cv_ref[:, cols]
        kw = [k_ref[win, cols] for win in windows]
        vw = [v_ref[win, cols] for win in windows]
        out = None
        for hh, msk in enumerate(masks):
            qm = jnp.where(msk, q, zero)
            s_cx = _nt_dot(qm, ck)
            s_nb = []
            for rr in range(NBR_ROWS):
                first = NA_KH - 1 - deltas[rr]
                bias = jnp.concatenate([tile_ref[2 * p + hh, first + 2 * jj] for jj in range(kh // 2)], axis=-1)
                s_nb.append(_nt_dot(qm[rr * w:(rr + 1) * w, :], kw[rr]) + bias)
            s_nb = jnp.concatenate(s_nb, axis=0)
            mx = jnp.maximum(jnp.max(s_nb, axis=-1, keepdims=True), jnp.max(s_cx, axis=-1, keepdims=True))
            e_nb = jnp.exp(s_nb - mx)
            e_cx = jnp.exp(s_cx - mx)
            den = jnp.sum(e_nb, axis=-1, keepdims=True) + jnp.sum(e_cx, axis=-1, keepdims=True)
            e_nb = e_nb.astype(BF16)
            o_nb = jnp.concatenate(
                [jnp.dot(e_nb[rr * w:(rr + 1) * w, :], jnp.where(msk, vw[rr], zero), preferred_element_type=F32)
                 for rr in range(NBR_ROWS)], axis=0)
            o = jnp.dot(e_cx.astype(BF16), jnp.where(msk, cv, zero), preferred_element_type=F32)
            o = (o + o_nb) / den
            out = o if out is None else out + o
        o_ref[:, cols] = out.astype(o_ref.dtype)


def neighbourhood_attention(q, k, v, ctx_k, ctx_v, tiles, layer, seq_len, head_dim):
    m, width = q.shape
    n_seq = m // seq_len
    rows = seq_len // GRID_W
    kh = min(NA_KH, rows)
    assert kh % 2 == 0 and rows % NBR_ROWS == 0
    groups = rows // NBR_ROWS
    past = ctx_k.shape[2]
    return pl.pallas_call(
        functools.partial(_nbr_attn_kernel, head_dim=head_dim, rows=rows, kh=kh),
        out_shape=jax.ShapeDtypeStruct((m, width), BF16),
        grid=(n_seq, groups),
        in_specs=[
            pl.BlockSpec((NBR_ROWS * GRID_W, width), lambda b, g: (b * groups + g, 0)),
            pl.BlockSpec((seq_len, width), lambda b, g: (b, 0)),
            pl.BlockSpec((seq_len, width), lambda b, g: (b, 0)),
            pl.BlockSpec((None, None, past, width), lambda b, g: (b, layer, 0, 0)),
            pl.BlockSpec((None, None, past, width), lambda b, g: (b, layer, 0, 0)),
            pl.BlockSpec((None,) + tiles.shape[1:], lambda b, g: (layer, 0, 0, 0, 0)),
        ],
        out_specs=pl.BlockSpec((NBR_ROWS * GRID_W, width), lambda b, g: (b * groups + g, 0)),
        compiler_params=_params("parallel", "parallel"),
        name="nbr_attn",
    )(q, k, v, ctx_k, ctx_v, tiles)


def _out_proj_kernel(y0_ref, y1_ref, y2_ref, y3_ref, w_ref, x_ref, mod_ref, g_ref, o_ref, *, d, gw):
    gg = g_ref[...] * mod_ref[:, 2 * d:3 * d]
    for r in range(x_ref.shape[0] // ROW_CHUNK):
        rows = slice(r * ROW_CHUNK, (r + 1) * ROW_CHUNK)
        mix = None
        for gi, y_ref in enumerate((y0_ref, y1_ref, y2_ref, y3_ref)):
            part = jnp.dot(y_ref[rows, :], w_ref[gi * gw:(gi + 1) * gw, :], preferred_element_type=F32)
            mix = part if mix is None else mix + part
        rs = lax.rsqrt(jnp.mean(mix * mix, axis=-1, keepdims=True) + EPS)
        o_ref[rows, :] = x_ref[rows, :] + mix * rs * gg


def out_proj(ys, w_out, x, mods, g_post, layer, seq_len, cond_base, cond_stride, tm=512):
    m, d = x.shape
    gw = ys[0].shape[-1]
    tm = min(tm, m)
    cidx = functools.partial(_cond_index, tm=tm, seq_len=seq_len, cond_base=cond_base, cond_stride=cond_stride)
    yspec = pl.BlockSpec((tm, gw), lambda i: (i, 0))
    return pl.pallas_call(
        functools.partial(_out_proj_kernel, d=d, gw=gw),
        out_shape=jax.ShapeDtypeStruct((m, d), F32),
        grid=(m // tm,),
        in_specs=[
            yspec, yspec, yspec, yspec,
            pl.BlockSpec((None, w_out.shape[1], d), lambda i: (layer, 0, 0)),
            pl.BlockSpec((tm, d), lambda i: (i, 0)),
            pl.BlockSpec((None, None, 1, N_MOD * d), lambda i: (layer, cidx(i), 0, 0)),
            pl.BlockSpec((None, 1, d), lambda i: (layer, 0, 0)),
        ],
        out_specs=pl.BlockSpec((tm, d), lambda i: (i, 0)),
        compiler_params=_params("parallel"),
        name="out_proj",
    )(*ys, w_out, x, mods, g_post)


def _ffn_kernel(x_ref, mod_ref, gpre_ref, gpost_ref, w1_ref, w3_ref, w2_ref, o_ref, h_ref, *, d, rb):
    j = pl.program_id(1)

    @pl.when(j == 0)
    def _():
        _norm_modulate(x_ref, gpre_ref[...], mod_ref[:, 3 * d:4 * d], mod_ref[:, 4 * d:5 * d], h_ref)

    def step(first):
        w1, w3, w2 = w1_ref[...], w3_ref[...], w2_ref[...].astype(BF16)
        for r in range(h_ref.shape[0] // rb):
            rows = slice(r * rb, (r + 1) * rb)
            h = h_ref[rows, :]
            a = jnp.dot(h, w1, preferred_element_type=F32)
            b = jnp.dot(h, w3, preferred_element_type=F32)
            hid = (a * _sigmoid(a) * b).astype(BF16)
            part = jnp.dot(hid, w2, preferred_element_type=F32)
            if first:
                o_ref[rows, :] = part
            else:
                o_ref[rows, :] += part

    last = pl.num_programs(1) - 1

    @pl.when(j == 0)
    def _():
        step(True)

    @pl.when((j > 0) & (j < last))
    def _():
        step(False)

    @pl.when(j == last)
    def _():
        step(False)
        _gated_residual(x_ref, o_ref, gpost_ref[...], mod_ref[:, 5 * d:6 * d], o_ref)


def ffn(x, mods, g_pre, g_post, w1, w3, w2, layer, seq_len, cond_base, cond_stride, tm=1024, rb=512):
    m, d = x.shape
    n_slices, tf = w1.shape[1], w1.shape[-1]
    tm = min(tm, m)
    rb = min(rb, tm)
    cidx = functools.partial(_cond_index, tm=tm, seq_len=seq_len, cond_base=cond_base, cond_stride=cond_stride)
    return pl.pallas_call(
        functools.partial(_ffn_kernel, d=d, rb=rb),
        out_shape=jax.ShapeDtypeStruct((m, d), F32),
        grid=(m // tm, n_slices),
        in_specs=[
            pl.BlockSpec((tm, d), lambda i, j: (i, 0)),
            pl.BlockSpec((None, None, 1, N_MOD * d), lambda i, j: (layer, cidx(i), 0, 0)),
            pl.BlockSpec((None, 1, d), lambda i, j: (layer, 0, 0)),
            pl.BlockSpec((None, 1, d), lambda i, j: (layer, 0, 0)),
            pl.BlockSpec((None, None, d, tf), lambda i, j: (layer, j, 0, 0)),
            pl.BlockSpec((None, None, d, tf), lambda i, j: (layer, j, 0, 0)),
            pl.BlockSpec((None, tf, d), lambda i, j: (layer, j, 0)),
        ],
        out_specs=pl.BlockSpec((tm, d), lambda i, j: (i, 0)),
        scratch_shapes=[pltpu.VMEM((tm, d), BF16)],
        compiler_params=_params("parallel", "arbitrary"),
        name="ffn",
    )(x, mods, g_pre, g_post, w1, w3, w2)


def _block_diag_gates(gate_w):
    depth, _, _, nblk, c, _ = gate_w.shape
    per = LANES // c
    w = gate_w.reshape(depth, 4, nblk // per, per, c, c).astype(BF16)
    eye = jnp.eye(per, dtype=BF16)
    tiles = jnp.einsum('lgnpce,pq->lgnpcqe', w, eye)
    return tiles.reshape(depth, 4, nblk // per, LANES, LANES)


FFN_SLICE = 512


def _hidden_slices(w):
    depth, d, f = w.shape
    return jnp.transpose(w.astype(BF16).reshape(depth, d, f // FFN_SLICE, FFN_SLICE), (0, 2, 1, 3))


def kernel(x_prompt, x_sample, cache_k, cache_v, state_lru, c, c_ctx, w_ada, b_ada, g_pre_mix, g_post_mix,
           g_pre_ffn, g_post_ffn, w_in, pool_w, pool_scale, lru_conv_w, lru_conv_b, lru_gate_w, lru_gate_b,
           lru_lambda, cm_dw_w, cm_dw_b, cm_ln_g, cm_ln_b, cm_pw_w, cm_pw_b, na_rpb, w_out, ffn_w1, ffn_w3, ffn_w2):
    batch, seq, d = x_prompt.shape
    dec_batch, dec_seq, _ = x_sample.shape
    depth = w_in.shape[0]
    gw = pool_scale.shape[-1]
    heads, head_dim = cache_k.shape[-2], cache_k.shape[-1]
    past = cache_k.shape[2]
    assert dec_batch + 1 <= 8 and seq % SEQ_CHUNK == 0 and dec_seq % SEQ_CHUNK == 0

    cond = jnp.zeros((8, d), F32).at[0].set(c_ctx).at[1:1 + dec_batch].set(c)
    mods = adaln_all(cond, w_ada, b_ada).reshape(depth, 8, 1, N_MOD * d)

    vec = lambda a: a.reshape(depth, 1, a.shape[-1])
    g_pre_mix, g_post_mix, g_pre_ffn, g_post_ffn = map(vec, (g_pre_mix, g_post_mix, g_pre_ffn, g_post_ffn))
    pool_scale, lru_conv_b, cm_dw_b, cm_ln_g, cm_ln_b, cm_pw_b = map(
        vec, (pool_scale, lru_conv_b, cm_dw_b, cm_ln_g, cm_ln_b, cm_pw_b))
    w_in_b, w_out_b = w_in.astype(BF16), w_out.astype(BF16)
    w1_s, w3_s = _hidden_slices(ffn_w1), _hidden_slices(ffn_w3)
    pool_w_b, cm_pw_b16 = pool_w.astype(BF16), cm_pw_w.astype(BF16)
    gate_tiles = _block_diag_gates(lru_gate_w)
    gate_b = lru_gate_b.reshape(depth, 4, gw)
    tiles = rpb_tiles(na_rpb)
    ctx_k = cache_k.reshape(dec_batch, depth, past, gw).astype(BF16)
    ctx_v = cache_v.reshape(dec_batch, depth, past, gw).astype(BF16)

    def layer_step(x, l, seq_len, cond_base, cond_stride, h0, kv):
        res = in_proj(x, mods, g_pre_mix, w_in_b, l, seq_len, cond_base, cond_stride, kv=kv)
        u_pool, u_lru_x, u_lru_g, u_cm_a, u_cm_g, u_q, u_k, u_v = res[:8]
        y_pool, y_cm = pool_conformer(u_pool, u_cm_a, u_cm_g, pool_w_b, pool_scale, cm_dw_w, cm_dw_b, cm_ln_g, cm_ln_b,
                                      cm_pw_b16, cm_pw_b, l, seq_len)
        y_lru, st = lru_mixer(u_lru_x, u_lru_g, lru_conv_w, lru_conv_b, gate_tiles, gate_b, lru_lambda, h0, l, seq_len)
        if kv is None:
            y_att = neighbourhood_attention(u_q, u_k, u_v, ctx_k, ctx_v, tiles, l, seq_len, head_dim)
        else:
            y_att = context_attention(u_q, u_k, u_v, seq_len, head_dim)
        x = out_proj((y_pool, y_lru, y_cm, y_att), w_out_b, x, mods, g_post_mix, l, seq_len, cond_base, cond_stride)
        x = ffn(x, mods, g_pre_ffn, g_post_ffn, w1_s, w3_s, ffn_w2, l, seq_len, cond_base, cond_stride)
        return x, st, tuple(res[8:])

    xp = x_prompt.reshape(batch * seq, d)
    xs = x_sample.reshape(dec_batch * dec_seq, d)
    zeros_h0 = jnp.zeros((batch, 2, gw), F32)
    kv = tuple(jnp.zeros((batch, depth, seq, gw), F32) for _ in range(2))
    sts = []
    for l in range(depth):
        xp, st, kv = layer_step(xp, l, seq, 0, 0, zeros_h0, kv)
        sts.append(st)
        xs, _, _ = layer_step(xs, l, dec_seq, 1, 1, state_lru[:, l], None)

    out_kv = (batch, depth, seq, heads, head_dim)
    return (xp.reshape(batch, seq, d), xs.reshape(dec_batch, dec_seq, d),
            kv[0].reshape(out_kv), kv[1].reshape(out_kv), jnp.stack(sts, axis=1))
```

```python
import functools

import jax
import jax.numpy as jnp
from jax import lax
from jax.experimental import pallas as pl
from jax.experimental.pallas import tpu as pltpu

F32 = jnp.float32
BF16 = jnp.bfloat16

EPS = 1e-6
NEG_INF = -1e30
N_MOD = 6
POOL_WINDOWS = (2, 4, 8, 16)
LRU_CONV = 4
LRU_C = 8.0
CM_CONV = 31
NA_KH = 8
NA_KW = 16
GRID_W = 64

LANES = 128
SUBLANES = 8
VMEM_LIMIT_BYTES = 60 * 1024 * 1024
SEQ_CHUNK = 256
HALO = 16


def _params(*sem):
    return pltpu.CompilerParams(dimension_semantics=sem, vmem_limit_bytes=VMEM_LIMIT_BYTES)


def _sigmoid(x):
    return 0.5 * jnp.tanh(0.5 * x) + 0.5


def _adaln_kernel(cond_ref, w_ref, b_ref, o_ref):
    c = cond_ref[...]
    s = (c * _sigmoid(c)).astype(BF16)
    o_ref[...] = jnp.dot(s, w_ref[...].astype(BF16), preferred_element_type=F32) + b_ref[...]


def adaln_all(cond, w_ada, b_ada, tn=1024):
    depth, d, n = w_ada.shape
    return pl.pallas_call(
        _adaln_kernel,
        out_shape=jax.ShapeDtypeStruct((depth, 8, n), F32),
        grid=(depth, n // tn),
        in_specs=[
            pl.BlockSpec((8, d), lambda l, j: (0, 0)),
            pl.BlockSpec((None, d, tn), lambda l, j: (l, 0, j)),
            pl.BlockSpec((None, 1, tn), lambda l, j: (l, 0, j)),
        ],
        out_specs=pl.BlockSpec((None, 8, tn), lambda l, j: (l, 0, j)),
        compiler_params=_params("parallel", "parallel"),
        name="adaln",
    )(cond, w_ada, b_ada.reshape(depth, 1, n))


ROW_CHUNK = 128


def _norm_modulate(x_ref, g, shift, scale, h_ref):
    gs = g * (1.0 + scale)
    for r in range(x_ref.shape[0] // ROW_CHUNK):
        rows = slice(r * ROW_CHUNK, (r + 1) * ROW_CHUNK)
        x = x_ref[rows, :]
        rs = lax.rsqrt(jnp.mean(x * x, axis=-1, keepdims=True) + EPS)
        h_ref[rows, :] = (x * rs * gs + shift).astype(h_ref.dtype)


def _gated_residual(x_ref, y_ref, g, gate, o_ref):
    gg = g * gate
    for r in range(x_ref.shape[0] // ROW_CHUNK):
        rows = slice(r * ROW_CHUNK, (r + 1) * ROW_CHUNK)
        y = y_ref[rows, :]
        rs = lax.rsqrt(jnp.mean(y * y, axis=-1, keepdims=True) + EPS)
        o_ref[rows, :] = x_ref[rows, :] + y * rs * gg


def _in_proj_kernel(x_ref, mod_ref, g_ref, w_ref, *rest, d, tn, with_kv):
    if with_kv:
        rest = rest[2:]
        k_ref, v_ref = rest[-3:-1]
    h_ref = rest[-1]
    n = w_ref.shape[-1]
    half = tn // 2
    u_refs = rest[:n // half]
    _norm_modulate(x_ref, g_ref[...], mod_ref[:, 0:d], mod_ref[:, d:2 * d], h_ref)
    for c in range(n // tn):
        res = jnp.dot(h_ref[...], w_ref[:, c * tn:(c + 1) * tn], preferred_element_type=F32)
        u_refs[2 * c][...] = res[:, :half].astype(BF16)
        u_refs[2 * c + 1][...] = res[:, half:].astype(BF16)
        if with_kv and c == n // tn - 1:
            k_ref[...] = res[:, :half].reshape(k_ref.shape)
            v_ref[...] = res[:, half:].reshape(v_ref.shape)


def _cond_index(i, tm, seq_len, cond_base, cond_stride):
    assert cond_stride == 0 or seq_len % tm == 0, "a row tile must not straddle two conditioning vectors"
    return cond_base + ((i * tm) // seq_len) * cond_stride


def in_proj(x, mods, g_pre, w_in, layer, seq_len, cond_base, cond_stride, kv=None, tm=512, tn=1024):
    m, d = x.shape
    n = w_in.shape[-1]
    tm = min(tm, m)
    assert n % tn == 0
    cidx = functools.partial(_cond_index, tm=tm, seq_len=seq_len, cond_base=cond_base, cond_stride=cond_stride)
    in_specs = [
        pl.BlockSpec((tm, d), lambda i: (i, 0)),
        pl.BlockSpec((None, None, 1, N_MOD * d), lambda i: (layer, cidx(i), 0, 0)),
        pl.BlockSpec((None, 1, d), lambda i: (layer, 0, 0)),
        pl.BlockSpec((None, d, n), lambda i: (layer, 0, 0), pipeline_mode=pl.Buffered(1)),
    ]
    n_out = 2 * n // tn
    out_shape = [jax.ShapeDtypeStruct((m, tn // 2), BF16)] * n_out
    out_specs = [pl.BlockSpec((tm, tn // 2), lambda i: (i, 0))] * n_out
    args = [x, mods, g_pre, w_in]
    aliases = {}
    if kv is not None:
        width = kv[0].shape[-1]
        assert tn == 2 * width and tm % seq_len == 0
        kv_spec = pl.BlockSpec((tm // seq_len, None, seq_len, width), lambda i: (i, layer, 0, 0))
        in_specs += [pl.BlockSpec(memory_space=pl.ANY)] * 2
        out_shape += [jax.ShapeDtypeStruct(kv[0].shape, F32)] * 2
        out_specs += [kv_spec, kv_spec]
        args += list(kv)
        aliases = {4: n_out, 5: n_out + 1}
    return pl.pallas_call(
        functools.partial(_in_proj_kernel, d=d, tn=tn, with_kv=kv is not None),
        out_shape=out_shape,
        grid=(m // tm,),
        in_specs=in_specs,
        out_specs=out_specs,
        scratch_shapes=[pltpu.VMEM((tm, d), BF16)],
        input_output_aliases=aliases,
        compiler_params=_params("parallel"),
        name="in_proj",
    )(*args)


def _halo_maps(seq_len, n_rows):
    chunks = seq_len // SEQ_CHUNK
    per = SEQ_CHUNK // HALO
    nblk = n_rows // HALO

    def prev_map(s, c):
        return (jnp.maximum((s * chunks + c) * per - 1, 0), 0)

    def next_map(s, c):
        return (jnp.minimum((s * chunks + c + 1) * per, nblk - 1), 0)

    return prev_map, next_map


def _pool_body(prev_ref, cur_ref, next_ref, w_ref, sc_ref, o_ref, pad_ref, *, chunks, seq_len):
    c = pl.program_id(1)
    cur = cur_ref[...].astype(F32)
    pad_ref[0:HALO, :] = jnp.where(c == 0, 0.0, prev_ref[...].astype(F32))
    pad_ref[HALO:HALO + SEQ_CHUNK, :] = cur
    pad_ref[HALO + SEQ_CHUNK:2 * HALO + SEQ_CHUNK, :] = jnp.where(c == chunks - 1, 0.0, next_ref[...].astype(F32))
    t = c * SEQ_CHUNK + lax.broadcasted_iota(jnp.int32, (SEQ_CHUNK, LANES), 0)
    for gi, w in enumerate(POOL_WINDOWS):
        cols = slice(gi * LANES, (gi + 1) * LANES)
        acc = pad_ref[HALO - w // 2:HALO - w // 2 + SEQ_CHUNK, cols]
        for off in range(-(w // 2) + 1, w - w // 2):
            acc = acc + pad_ref[HALO + off:HALO + off + SEQ_CHUNK, cols]
        lo = jnp.clip(t - w // 2, 0, seq_len)
        hi = jnp.clip(t - w // 2 + w, 0, seq_len)
        pooled = acc / (hi - lo).astype(F32) - cur[:, cols]
        y = jnp.dot(pooled.astype(BF16), w_ref[gi], preferred_element_type=F32) * sc_ref[:, cols]
        o_ref[:, cols] = y.astype(o_ref.dtype)


def _conformer_body(ap_ref, a_ref, an_ref, gp_ref, g_ref, gn_ref, dw_ref, dwb_ref, lng_ref, lnb_ref,
                    pw_ref, pwb_ref, o_ref, pad_ref, sh_ref, *, chunks):
    c = pl.program_id(1)
    glu = lambda a, g: a[...].astype(F32) * _sigmoid(g[...].astype(F32))
    pad_ref[0:HALO, :] = jnp.where(c == 0, 0.0, glu(ap_ref, gp_ref))
    pad_ref[HALO:HALO + SEQ_CHUNK, :] = glu(a_ref, g_ref)
    pad_ref[HALO + SEQ_CHUNK:2 * HALO + SEQ_CHUNK, :] = jnp.where(c == chunks - 1, 0.0, glu(an_ref, gn_ref))
    n_sh = sh_ref.shape[1]
    for r in range(1, SUBLANES):
        sh_ref[r - 1] = pad_ref[r:r + n_sh, :]
    first = HALO - CM_CONV // 2
    acc = None
    for k in range(CM_CONV):
        q, r = divmod(first + k, SUBLANES)
        rows = slice(q * SUBLANES, q * SUBLANES + SEQ_CHUNK)
        tap = dw_ref[k:k + 1, :] * (pad_ref[rows, :] if r == 0 else sh_ref[r - 1, rows, :])
        acc = tap + dwb_ref[...] if acc is None else acc + tap
    mu = jnp.mean(acc, axis=-1, keepdims=True)
    cen = acc - mu
    var = jnp.mean(cen * cen, axis=-1, keepdims=True)
    y = cen * lax.rsqrt(var + EPS) * lng_ref[...] + lnb_ref[...]
    z = (y * _sigmoid(y)).astype(BF16)
    o = jnp.dot(z, pw_ref[...], preferred_element_type=F32) + pwb_ref[...]
    o_ref[...] = o.astype(o_ref.dtype)


def _pool_conformer_kernel(*refs, chunks, seq_len):
    pool_in, cm_in = refs[0:5], refs[5:17]
    pool_out, cm_out, pool_pad, cm_pad, cm_sh = refs[17:22]
    _pool_body(*pool_in, pool_out, pool_pad, chunks=chunks, seq_len=seq_len)
    _conformer_body(*cm_in, cm_out, cm_pad, cm_sh, chunks=chunks)


def pool_conformer(u_pool, u_a, u_g, pool_w, pool_scale, dw_w, dw_b, ln_g, ln_b, pw_w, pw_b, layer, seq_len):
    m = u_pool.shape[0]
    width = dw_w.shape[-1]
    assert width == len(POOL_WINDOWS) * LANES
    n_seq = m // seq_len
    chunks = seq_len // SEQ_CHUNK
    padded = SEQ_CHUNK + 2 * HALO
    vec = lambda: pl.BlockSpec((None, 1, width), lambda s, c: (layer, 0, 0))

    prev_map, next_map = _halo_maps(seq_len, m)
    chunk_specs = [pl.BlockSpec((HALO, width), prev_map),
                   pl.BlockSpec((SEQ_CHUNK, width), lambda s, c: (s * chunks + c, 0)),
                   pl.BlockSpec((HALO, width), next_map)]
    out_spec = pl.BlockSpec((SEQ_CHUNK, width), lambda s, c: (s * chunks + c, 0))
    return pl.pallas_call(
        functools.partial(_pool_conformer_kernel, chunks=chunks, seq_len=seq_len),
        out_shape=(jax.ShapeDtypeStruct((m, width), BF16),) * 2,
        grid=(n_seq, chunks),
        in_specs=chunk_specs + [
            pl.BlockSpec((None, len(POOL_WINDOWS), LANES, LANES), lambda s, c: (layer, 0, 0, 0)),
            vec(),
        ] + chunk_specs + chunk_specs + [
            pl.BlockSpec((None, CM_CONV, width), lambda s, c: (layer, 0, 0)),
            vec(), vec(), vec(),
            pl.BlockSpec((None, width, width), lambda s, c: (layer, 0, 0)),
            vec(),
        ],
        out_specs=(out_spec, out_spec),
        scratch_shapes=[pltpu.VMEM((padded, width), F32), pltpu.VMEM((padded, width), F32),
                        pltpu.VMEM((SUBLANES - 1, padded - SUBLANES, width), F32)],
        compiler_params=_params("parallel", "parallel"),
        name="pool_conformer",
    )(u_pool, u_pool, u_pool, pool_w, pool_scale, u_a, u_a, u_a, u_g, u_g, u_g, dw_w, dw_b, ln_g, ln_b, pw_w, pw_b)


def _softplus(z):
    return jnp.maximum(z, 0.0) + jnp.log1p(jnp.exp(-jnp.abs(z)))


def _gelu_tanh(x):
    return 0.5 * x * (1.0 + jnp.tanh(0.7978845608028654 * (x + 0.044715 * (x * x * x))))


def _row_select(rows):
    sub = lax.broadcasted_iota(jnp.int32, (SUBLANES, LANES), 0)
    out = jnp.zeros((SUBLANES, LANES), F32)
    for s, r in enumerate(rows):
        out = jnp.where(sub == s, r, out)
    return out


def _lru_kernel(x_ref, g_ref, cw_ref, cb_ref, gw_ref, gb_ref, lam_ref, h0_ref, y_ref, st_ref,
                xpad, a_f, b_f, a_b, b_b, *, seq_len):
    t_len = seq_len
    nb = x_ref.shape[-1] // LANES
    seg = t_len // SUBLANES
    pitch = seg + 4
    width = x_ref.shape[-1]
    xpad[0:SUBLANES, :] = jnp.zeros((SUBLANES, width), F32)
    xpad[SUBLANES + t_len:2 * SUBLANES + t_len, :] = jnp.zeros((SUBLANES, width), F32)
    xpad[SUBLANES:SUBLANES + t_len, :] = x_ref[...].astype(F32)
    left = LRU_CONV // 2
    sp = _softplus(-lam_ref[...])
    for lb in range(nb):
        cols = slice(lb * LANES, (lb + 1) * LANES)
        x = cb_ref[:, cols] + cw_ref[0:1, cols] * xpad[SUBLANES - left:SUBLANES - left + t_len, cols]
        for k in range(1, LRU_CONV):
            x = x + cw_ref[k:k + 1, cols] * xpad[SUBLANES - left + k:SUBLANES - left + k + t_len, cols]
        xb = x.astype(BF16)
        for d, (a_ref, b_ref) in enumerate(((a_f, b_f), (a_b, b_b))):
            gate = lambda gi: _sigmoid(jnp.dot(xb, gw_ref[gi, lb], preferred_element_type=F32) + gb_ref[gi:gi + 1, cols])
            r, i = gate(2 * d), gate(2 * d + 1)
            log_a = (-LRU_C) * r * sp[d:d + 1, cols]
            a = jnp.exp(log_a)
            var = jnp.maximum(-jnp.tanh(log_a) * (a * a + 1.0), 0.0)
            bt = jnp.where(var > 0.0, var * lax.rsqrt(var), 0.0) * (i * x)
            for s in range(SUBLANES):
                a_ref[lb, s * pitch:s * pitch + seg, :] = a[s * seg:(s + 1) * seg, :]
                b_ref[lb, s * pitch:s * pitch + seg, :] = bt[s * seg:(s + 1) * seg, :]

    def rows(ref, lb, j):
        return ref[lb, pl.ds(j, SUBLANES, stride=pitch), :]

    zeros = jnp.zeros((SUBLANES, LANES), F32)
    ones = jnp.ones((SUBLANES, LANES), F32)

    def totals(j, carry):
        out = []
        jb = seg - 1 - j
        for lb in range(nb):
            hf, pf, hb, pb = carry[4 * lb:4 * lb + 4]
            af, ab = rows(a_f, lb, j), rows(a_b, lb, jb)
            out += [af * hf + rows(b_f, lb, j), af * pf, ab * hb + rows(b_b, lb, jb), ab * pb]
        return tuple(out)

    tot = lax.fori_loop(0, seg, totals, (zeros, ones, zeros, ones) * nb)

    cins = []
    for lb in range(nb):
        cols = slice(lb * LANES, (lb + 1) * LANES)
        hf, pf, hb, pb = tot[4 * lb:4 * lb + 4]
        cf = [h0_ref[0:1, cols]]
        for s in range(SUBLANES):
            cf.append(hf[s:s + 1, :] + pf[s:s + 1, :] * cf[-1])
        cb = [h0_ref[1:2, cols]]
        for s in reversed(range(SUBLANES)):
            cb.append(hb[s:s + 1, :] + pb[s:s + 1, :] * cb[-1])
        st_ref[0:1, cols] = cf[SUBLANES]
        st_ref[1:2, cols] = cb[SUBLANES]
        cins += [_row_select(cf[:SUBLANES]), _row_select(list(reversed(cb[:SUBLANES])))]

    def states(j, carry):
        out = []
        jb = seg - 1 - j
        for lb in range(nb):
            hf = rows(a_f, lb, j) * carry[2 * lb] + rows(b_f, lb, j)
            hb = rows(a_b, lb, jb) * carry[2 * lb + 1] + rows(b_b, lb, jb)
            b_f[lb, pl.ds(j, SUBLANES, stride=pitch), :] = hf
            b_b[lb, pl.ds(jb, SUBLANES, stride=pitch), :] = hb
            out += [hf, hb]
        return tuple(out)

    lax.fori_loop(0, seg, states, tuple(cins))

    for lb in range(nb):
        cols = slice(lb * LANES, (lb + 1) * LANES)
        for s in range(SUBLANES):
            h = b_f[lb, s * pitch:s * pitch + seg, :] + b_b[lb, s * pitch:s * pitch + seg, :]
            gate = _gelu_tanh(g_ref[s * seg:(s + 1) * seg, cols].astype(F32))
            y_ref[s * seg:(s + 1) * seg, cols] = (h * gate).astype(y_ref.dtype)


def lru_mixer(u_x, u_g, conv_w, conv_b, gate_w, gate_b, lam, h0, layer, seq_len):
    m = u_x.shape[0]
    width = conv_w.shape[-1]
    nb = width // LANES
    n_seq = m // seq_len
    scan_buf = pltpu.VMEM((nb, SUBLANES * (seq_len // SUBLANES + 4), LANES), F32)
    return pl.pallas_call(
        functools.partial(_lru_kernel, seq_len=seq_len),
        out_shape=(jax.ShapeDtypeStruct((m, width), BF16), jax.ShapeDtypeStruct((n_seq, 2, width), F32)),
        grid=(n_seq,),
        in_specs=[
            pl.BlockSpec((seq_len, width), lambda s: (s, 0)),
            pl.BlockSpec((seq_len, width), lambda s: (s, 0)),
            pl.BlockSpec((None, LRU_CONV, width), lambda s: (layer, 0, 0)),
            pl.BlockSpec((None, 1, width), lambda s: (layer, 0, 0)),
            pl.BlockSpec((None, 4, nb, LANES, LANES), lambda s: (layer, 0, 0, 0, 0)),
            pl.BlockSpec((None, 4, width), lambda s: (layer, 0, 0)),
            pl.BlockSpec((None, 2, width), lambda s: (layer, 0, 0)),
            pl.BlockSpec((None, 2, width), lambda s: (s, 0, 0)),
        ],
        out_specs=(
            pl.BlockSpec((seq_len, width), lambda s: (s, 0)),
            pl.BlockSpec((None, 2, width), lambda s: (s, 0, 0)),
        ),
        scratch_shapes=[pltpu.VMEM((seq_len + 2 * SUBLANES, width), F32), scan_buf, scan_buf, scan_buf, scan_buf],
        compiler_params=_params("parallel"),
        name="lru",
    )(u_x, u_g, conv_w, conv_b, gate_w, gate_b, lam, h0)


def _head_masks():
    lane = lax.broadcasted_iota(jnp.int32, (1, LANES), 1)
    return lane < LANES // 2, lane >= LANES // 2


def _nt_dot(a, b):
    return lax.dot_general(a, b, (((1,), (1,)), ((), ())), preferred_element_type=F32)


def _scaled(q, scale):
    return (q.astype(F32) * scale).astype(BF16)


CTX_SEQS = 2


def _ctx_attn_kernel(q_ref, k_ref, v_ref, o_ref, *, head_dim, seq_len):
    masks = _head_masks()
    zero = jnp.zeros((), BF16)
    for sq in range(q_ref.shape[0] // seq_len):
        rows = slice(sq * seq_len, (sq + 1) * seq_len)
        for p in range(q_ref.shape[-1] // LANES):
            cols = slice(p * LANES, (p + 1) * LANES)
            q = _scaled(q_ref[rows, cols], head_dim ** -0.5)
            k = k_ref[rows, cols]
            v = v_ref[rows, cols]
            out = None
            for msk in masks:
                s = _nt_dot(jnp.where(msk, q, zero), k)
                e = jnp.exp(s - jnp.max(s, axis=-1, keepdims=True))
                den = jnp.sum(e, axis=-1, keepdims=True)
                o = jnp.dot(e.astype(BF16), jnp.where(msk, v, zero), preferred_element_type=F32) / den
                out = o if out is None else out + o
            o_ref[rows, cols] = out.astype(o_ref.dtype)


def context_attention(q, k, v, seq_len, head_dim):
    m, width = q.shape
    n_seq = m // seq_len
    per_step = CTX_SEQS if n_seq % CTX_SEQS == 0 else 1
    spec = pl.BlockSpec((per_step * seq_len, width), lambda s: (s, 0))
    return pl.pallas_call(
        functools.partial(_ctx_attn_kernel, head_dim=head_dim, seq_len=seq_len),
        out_shape=jax.ShapeDtypeStruct((m, width), BF16),
        grid=(n_seq // per_step,),
        in_specs=[spec, spec, spec],
        out_specs=spec,
        compiler_params=_params("parallel"),
        name="ctx_attn",
    )(q, k, v)


def _rpb_tile_kernel(rpb_ref, o_ref):
    n_rows, nd = 2 * NA_KH - 1, 2 * NA_KW - 1
    qc = lax.broadcasted_iota(jnp.int32, (GRID_W, 2 * GRID_W), 0)
    lane = lax.broadcasted_iota(jnp.int32, (GRID_W, 2 * GRID_W), 1)
    kc = lane & (GRID_W - 1)
    start = jnp.clip(qc - NA_KW // 2, 0, GRID_W - NA_KW)
    inside = (kc >= start) & (kc < start + NA_KW)
    idx = jnp.clip(kc - qc + NA_KW - 1, 0, nd - 1)
    base = (pl.program_id(0) * pl.num_programs(1) + pl.program_id(1)) * n_rows
    tiles = []
    for r in range(n_rows):
        acc = jnp.zeros((GRID_W, 2 * GRID_W), F32)
        for dd in range(nd):
            acc = jnp.where(idx == dd, rpb_ref[(base + r) * nd + dd], acc)
        tiles.append(jnp.where(inside, acc, NEG_INF))
    for r in range(n_rows - 1):
        o_ref[r] = jnp.where(lane < GRID_W, tiles[r], tiles[r + 1])


def rpb_tiles(rpb):
    depth, heads, nr, _ = rpb.shape
    return pl.pallas_call(
        _rpb_tile_kernel,
        out_shape=jax.ShapeDtypeStruct((depth, heads, nr - 1, GRID_W, 2 * GRID_W), F32),
        grid=(depth, heads),
        in_specs=[pl.BlockSpec(memory_space=pltpu.SMEM)],
        out_specs=pl.BlockSpec((None, None, nr - 1, GRID_W, 2 * GRID_W), lambda l, h: (l, h, 0, 0, 0)),
        compiler_params=_params("parallel", "parallel"),
        name="rpb_tiles",
    )(rpb.reshape(-1))


NBR_ROWS = 4


def _nbr_attn_kernel(q_ref, k_ref, v_ref, ck_ref, cv_ref, tile_ref, o_ref, *, head_dim, rows, kh):
    masks = _head_masks()
    zero = jnp.zeros((), BF16)
    w = GRID_W
    windows, deltas = [], []
    for rr in range(NBR_ROWS):
        r = pl.program_id(1) * NBR_ROWS + rr
        r0 = jnp.clip(r - kh // 2, 0, rows - kh)
        windows.append(pl.ds(pl.multiple_of(r0 * w, w), kh * w))
        deltas.append(r - r0)
    for p in range(q_ref.shape[-1] // LANES):
        cols = slice(p * LANES, (p + 1) * LANES)
        q = _scaled(q_ref[:, cols], head_dim ** -0.5)
        ck = ck_ref[:, cols]
        cv = cv_ref[:, cols]
        kw = [k_ref[win, cols] for win in windows]
        vw = [v_ref[win, cols] for win in windows]
        out = None
        for hh, msk in enumerate(masks):
            qm = jnp.where(msk, q, zero)
            s_cx = _nt_dot(qm, ck)
            s_nb = []
            for rr in range(NBR_ROWS):
                first = NA_KH - 1 - deltas[rr]
                bias = jnp.concatenate([tile_ref[2 * p + hh, first + 2 * jj] for jj in range(kh // 2)], axis=-1)
                s_nb.append(_nt_dot(qm[rr * w:(rr + 1) * w, :], kw[rr]) + bias)
            s_nb = jnp.concatenate(s_nb, axis=0)
            mx = jnp.maximum(jnp.max(s_nb, axis=-1, keepdims=True), jnp.max(s_cx, axis=-1, keepdims=True))
            e_nb = jnp.exp(s_nb - mx)
            e_cx = jnp.exp(s_cx - mx)
            den = jnp.sum(e_nb, axis=-1, keepdims=True) + jnp.sum(e_cx, axis=-1, keepdims=True)
            e_nb = e_nb.astype(BF16)
            o_nb = jnp.concatenate(
                [jnp.dot(e_nb[rr * w:(rr + 1) * w, :], jnp.where(msk, vw[rr], zero), preferred_element_type=F32)
                 for rr in range(NBR_ROWS)], axis=0)
            o = jnp.dot(e_cx.astype(BF16), jnp.where(msk, cv, zero), preferred_element_type=F32)
            o = (o + o_nb) / den
            out = o if out is None else out + o
        o_ref[:, cols] = out.astype(o_ref.dtype)


def neighbourhood_attention(q, k, v, ctx_k, ctx_v, tiles, layer, seq_len, head_dim):
    m, width = q.shape
    n_seq = m // seq_len
    rows = seq_len // GRID_W
    kh = min(NA_KH, rows)
    assert kh % 2 == 0 and rows % NBR_ROWS == 0
    groups = rows // NBR_ROWS
    past = ctx_k.shape[2]
    return pl.pallas_call(
        functools.partial(_nbr_attn_kernel, head_dim=head_dim, rows=rows, kh=kh),
        out_shape=jax.ShapeDtypeStruct((m, width), BF16),
        grid=(n_seq, groups),
        in_specs=[
            pl.BlockSpec((NBR_ROWS * GRID_W, width), lambda b, g: (b * groups + g, 0)),
            pl.BlockSpec((seq_len, width), lambda b, g: (b, 0)),
            pl.BlockSpec((seq_len, width), lambda b, g: (b, 0)),
            pl.BlockSpec((None, None, past, width), lambda b, g: (b, layer, 0, 0)),
            pl.BlockSpec((None, None, past, width), lambda b, g: (b, layer, 0, 0)),
            pl.BlockSpec((None,) + tiles.shape[1:], lambda b, g: (layer, 0, 0, 0, 0)),
        ],
        out_specs=pl.BlockSpec((NBR_ROWS * GRID_W, width), lambda b, g: (b * groups + g, 0)),
        compiler_params=_params("parallel", "parallel"),
        name="nbr_attn",
    )(q, k, v, ctx_k, ctx_v, tiles)


def _out_proj_kernel(y0_ref, y1_ref, y2_ref, y3_ref, w_ref, x_ref, mod_ref, g_ref, o_ref, *, d, gw):
    gg = g_ref[...] * mod_ref[:, 2 * d:3 * d]
    for r in range(x_ref.shape[0] // ROW_CHUNK):
        rows = slice(r * ROW_CHUNK, (r + 1) * ROW_CHUNK)
        mix = None
        for gi, y_ref in enumerate((y0_ref, y1_ref, y2_ref, y3_ref)):
            part = jnp.dot(y_ref[rows, :], w_ref[gi * gw:(gi + 1) * gw, :], preferred_element_type=F32)
            mix = part if mix is None else mix + part
        rs = lax.rsqrt(jnp.mean(mix * mix, axis=-1, keepdims=True) + EPS)
        o_ref[rows, :] = x_ref[rows, :] + mix * rs * gg


def out_proj(ys, w_out, x, mods, g_post, layer, seq_len, cond_base, cond_stride, tm=1024):
    m, d = x.shape
    gw = ys[0].shape[-1]
    tm = min(tm, m)
    cidx = functools.partial(_cond_index, tm=tm, seq_len=seq_len, cond_base=cond_base, cond_stride=cond_stride)
    yspec = pl.BlockSpec((tm, gw), lambda i: (i, 0))
    return pl.pallas_call(
        functools.partial(_out_proj_kernel, d=d, gw=gw),
        out_shape=jax.ShapeDtypeStruct((m, d), F32),
        grid=(m // tm,),
        in_specs=[
            yspec, yspec, yspec, yspec,
            pl.BlockSpec((None, w_out.shape[1], d), lambda i: (layer, 0, 0), pipeline_mode=pl.Buffered(1)),
            pl.BlockSpec((tm, d), lambda i: (i, 0)),
            pl.BlockSpec((None, None, 1, N_MOD * d), lambda i: (layer, cidx(i), 0, 0)),
            pl.BlockSpec((None, 1, d), lambda i: (layer, 0, 0)),
        ],
        out_specs=pl.BlockSpec((tm, d), lambda i: (i, 0)),
        compiler_params=_params("parallel"),
        name="out_proj",
    )(*ys, w_out, x, mods, g_post)


def _ffn_kernel(x_ref, mod_ref, gpre_ref, gpost_ref, w1_ref, w3_ref, w2_ref, o_ref, h_ref, *, d, rb):
    j = pl.program_id(1)

    @pl.when(j == 0)
    def _():
        _norm_modulate(x_ref, gpre_ref[...], mod_ref[:, 3 * d:4 * d], mod_ref[:, 4 * d:5 * d], h_ref)

    def step(first):
        w1, w3, w2 = w1_ref[...], w3_ref[...], w2_ref[...].astype(BF16)
        for r in range(h_ref.shape[0] // rb):
            rows = slice(r * rb, (r + 1) * rb)
            h = h_ref[rows, :]
            a = jnp.dot(h, w1, preferred_element_type=F32)
            b = jnp.dot(h, w3, preferred_element_type=F32)
            hid = (a * _sigmoid(a) * b).astype(BF16)
            part = jnp.dot(hid, w2, preferred_element_type=F32)
            if first:
                o_ref[rows, :] = part
            else:
                o_ref[rows, :] += part

    last = pl.num_programs(1) - 1

    @pl.when(j == 0)
    def _():
        step(True)

    @pl.when((j > 0) & (j < last))
    def _():
        step(False)

    @pl.when(j == last)
    def _():
        step(False)
        _gated_residual(x_ref, o_ref, gpost_ref[...], mod_ref[:, 5 * d:6 * d], o_ref)


def ffn(x, mods, g_pre, g_post, w1, w3, w2, layer, seq_len, cond_base, cond_stride, tm=1024, tf=512, rb=512):
    m, d = x.shape
    f = w1.shape[-1]
    tm = min(tm, m)
    rb = min(rb, tm)
    cidx = functools.partial(_cond_index, tm=tm, seq_len=seq_len, cond_base=cond_base, cond_stride=cond_stride)
    return pl.pallas_call(
        functools.partial(_ffn_kernel, d=d, rb=rb),
        out_shape=jax.ShapeDtypeStruct((m, d), F32),
        grid=(m // tm, f // tf),
        in_specs=[
            pl.BlockSpec((tm, d), lambda i, j: (i, 0)),
            pl.BlockSpec((None, None, 1, N_MOD * d), lambda i, j: (layer, cidx(i), 0, 0)),
            pl.BlockSpec((None, 1, d), lambda i, j: (layer, 0, 0)),
            pl.BlockSpec((None, 1, d), lambda i, j: (layer, 0, 0)),
            pl.BlockSpec((None, d, tf), lambda i, j: (layer, 0, j)),
            pl.BlockSpec((None, d, tf), lambda i, j: (layer, 0, j)),
            pl.BlockSpec((None, tf, d), lambda i, j: (layer, j, 0)),
        ],
        out_specs=pl.BlockSpec((tm, d), lambda i, j: (i, 0)),
        scratch_shapes=[pltpu.VMEM((tm, d), BF16)],
        compiler_params=_params("parallel", "arbitrary"),
        name="ffn",
    )(x, mods, g_pre, g_post, w1, w3, w2)


def _block_diag_gates(gate_w):
    depth, _, _, nblk, c, _ = gate_w.shape
    per = LANES // c
    w = gate_w.reshape(depth, 4, nblk // per, per, c, c).astype(BF16)
    eye = jnp.eye(per, dtype=BF16)
    tiles = jnp.einsum('lgnpce,pq->lgnpcqe', w, eye)
    return tiles.reshape(depth, 4, nblk // per, LANES, LANES)


def kernel(x_prompt, x_sample, cache_k, cache_v, state_lru, c, c_ctx, w_ada, b_ada, g_pre_mix, g_post_mix,
           g_pre_ffn, g_post_ffn, w_in, pool_w, pool_scale, lru_conv_w, lru_conv_b, lru_gate_w, lru_gate_b,
           lru_lambda, cm_dw_w, cm_dw_b, cm_ln_g, cm_ln_b, cm_pw_w, cm_pw_b, na_rpb, w_out, ffn_w1, ffn_w3, ffn_w2):
    batch, seq, d = x_prompt.shape
    dec_batch, dec_seq, _ = x_sample.shape
    depth = w_in.shape[0]
    gw = pool_scale.shape[-1]
    heads, head_dim = cache_k.shape[-2], cache_k.shape[-1]
    past = cache_k.shape[2]
    assert dec_batch + 1 <= 8 and seq % SEQ_CHUNK == 0 and dec_seq % SEQ_CHUNK == 0

    cond = jnp.zeros((8, d), F32).at[0].set(c_ctx).at[1:1 + dec_batch].set(c)
    mods = adaln_all(cond, w_ada, b_ada).reshape(depth, 8, 1, N_MOD * d)

    vec = lambda a: a.reshape(depth, 1, a.shape[-1])
    g_pre_mix, g_post_mix, g_pre_ffn, g_post_ffn = map(vec, (g_pre_mix, g_post_mix, g_pre_ffn, g_post_ffn))
    pool_scale, lru_conv_b, cm_dw_b, cm_ln_g, cm_ln_b, cm_pw_b = map(
        vec, (pool_scale, lru_conv_b, cm_dw_b, cm_ln_g, cm_ln_b, cm_pw_b))
    w_in_b, w_out_b = w_in.astype(BF16), w_out.astype(BF16)
    w1_b, w3_b = ffn_w1.astype(BF16), ffn_w3.astype(BF16)
    pool_w_b, cm_pw_b16 = pool_w.astype(BF16), cm_pw_w.astype(BF16)
    gate_tiles = _block_diag_gates(lru_gate_w)
    gate_b = lru_gate_b.reshape(depth, 4, gw)
    tiles = rpb_tiles(na_rpb)
    ctx_k = cache_k.reshape(dec_batch, depth, past, gw).astype(BF16)
    ctx_v = cache_v.reshape(dec_batch, depth, past, gw).astype(BF16)

    def layer_step(x, l, seq_len, cond_base, cond_stride, h0, kv):
        res = in_proj(x, mods, g_pre_mix, w_in_b, l, seq_len, cond_base, cond_stride, kv=kv)
        u_pool, u_lru_x, u_lru_g, u_cm_a, u_cm_g, u_q, u_k, u_v = res[:8]
        y_pool, y_cm = pool_conformer(u_pool, u_cm_a, u_cm_g, pool_w_b, pool_scale, cm_dw_w, cm_dw_b, cm_ln_g, cm_ln_b,
                                      cm_pw_b16, cm_pw_b, l, seq_len)
        y_lru, st = lru_mixer(u_lru_x, u_lru_g, lru_conv_w, lru_conv_b, gate_tiles, gate_b, lru_lambda, h0, l, seq_len)
        if kv is None:
            y_att = neighbourhood_attention(u_q, u_k, u_v, ctx_k, ctx_v, tiles, l, seq_len, head_dim)
        else:
            y_att = context_attention(u_q, u_k, u_v, seq_len, head_dim)
        x = out_proj((y_pool, y_lru, y_cm, y_att), w_out_b, x, mods, g_post_mix, l, seq_len, cond_base, cond_stride)
        x = ffn(x, mods, g_pre_ffn, g_post_ffn, w1_b, w3_b, ffn_w2, l, seq_len, cond_base, cond_stride)
        return x, st, tuple(res[8:])

    xp = x_prompt.reshape(batch * seq, d)
    xs = x_sample.reshape(dec_batch * dec_seq, d)
    zeros_h0 = jnp.zeros((batch, 2, gw), F32)
    kv = tuple(jnp.zeros((batch, depth, seq, gw), F32) for _ in range(2))
    sts = []
    for l in range(depth):
        xp, st, kv = layer_step(xp, l, seq, 0, 0, zeros_h0, kv)
        sts.append(st)
        xs, _, _ = layer_step(xs, l, dec_seq, 1, 1, state_lru[:, l], None)

    out_kv = (batch, depth, seq, heads, head_dim)
    return (xp.reshape(batch, seq, d), xs.reshape(dec_batch, dec_seq, d),
            kv[0].reshape(out_kv), kv[1].reshape(out_kv), jnp.stack(sts, axis=1))
```

```python
import functools

import jax
import jax.numpy as jnp
from jax import lax
from jax.experimental import pallas as pl
from jax.experimental.pallas import tpu as pltpu

F32 = jnp.float32
BF16 = jnp.bfloat16

EPS = 1e-6
NEG_INF = -1e30
N_MOD = 6
POOL_WINDOWS = (2, 4, 8, 16)
LRU_CONV = 4
LRU_C = 8.0
CM_CONV = 31
NA_KH = 8
NA_KW = 16
GRID_W = 64

LANES = 128
SUBLANES = 8
VMEM_LIMIT_BYTES = 60 * 1024 * 1024
SEQ_CHUNK = 256
HALO = 16


def _params(*sem):
    return pltpu.CompilerParams(dimension_semantics=sem, vmem_limit_bytes=VMEM_LIMIT_BYTES)


def _sigmoid(x):
    return 0.5 * jnp.tanh(0.5 * x) + 0.5


def _adaln_kernel(cond_ref, w_ref, b_ref, o_ref):
    c = cond_ref[...]
    s = (c * _sigmoid(c)).astype(BF16)
    o_ref[...] = jnp.dot(s, w_ref[...].astype(BF16), preferred_element_type=F32) + b_ref[...]


def adaln_all(cond, w_ada, b_ada, tn=1024):
    depth, d, n = w_ada.shape
    return pl.pallas_call(
        _adaln_kernel,
        out_shape=jax.ShapeDtypeStruct((depth, 8, n), F32),
        grid=(depth, n // tn),
        in_specs=[
            pl.BlockSpec((8, d), lambda l, j: (0, 0)),
            pl.BlockSpec((None, d, tn), lambda l, j: (l, 0, j)),
            pl.BlockSpec((None, 1, tn), lambda l, j: (l, 0, j)),
        ],
        out_specs=pl.BlockSpec((None, 8, tn), lambda l, j: (l, 0, j)),
        compiler_params=_params("parallel", "parallel"),
        name="adaln",
    )(cond, w_ada, b_ada.reshape(depth, 1, n))


ROW_CHUNK = 128


def _norm_modulate(x_ref, g, shift, scale, h_ref):
    gs = g * (1.0 + scale)
    for r in range(x_ref.shape[0] // ROW_CHUNK):
        rows = slice(r * ROW_CHUNK, (r + 1) * ROW_CHUNK)
        x = x_ref[rows, :]
        rs = lax.rsqrt(jnp.mean(x * x, axis=-1, keepdims=True) + EPS)
        h_ref[rows, :] = (x * rs * gs + shift).astype(h_ref.dtype)


def _gated_residual(x_ref, y_ref, g, gate, o_ref):
    gg = g * gate
    for r in range(x_ref.shape[0] // ROW_CHUNK):
        rows = slice(r * ROW_CHUNK, (r + 1) * ROW_CHUNK)
        y = y_ref[rows, :]
        rs = lax.rsqrt(jnp.mean(y * y, axis=-1, keepdims=True) + EPS)
        o_ref[rows, :] = x_ref[rows, :] + y * rs * gg


def _in_proj_kernel(x_ref, mod_ref, g_ref, w_ref, *rest, d, tn, with_kv):
    if with_kv:
        rest = rest[2:]
        k_ref, v_ref = rest[-3:-1]
    h_ref = rest[-1]
    n = w_ref.shape[-1]
    half = tn // 2
    u_refs = rest[:n // half]
    _norm_modulate(x_ref, g_ref[...], mod_ref[:, 0:d], mod_ref[:, d:2 * d], h_ref)
    for c in range(n // tn):
        res = jnp.dot(h_ref[...], w_ref[:, c * tn:(c + 1) * tn], preferred_element_type=F32)
        u_refs[2 * c][...] = res[:, :half].astype(BF16)
        u_refs[2 * c + 1][...] = res[:, half:].astype(BF16)
        if with_kv and c == n // tn - 1:
            k_ref[...] = res[:, :half].reshape(k_ref.shape)
            v_ref[...] = res[:, half:].reshape(v_ref.shape)


def _cond_index(i, tm, seq_len, cond_base, cond_stride):
    assert cond_stride == 0 or seq_len % tm == 0, "a row tile must not straddle two conditioning vectors"
    return cond_base + ((i * tm) // seq_len) * cond_stride


def in_proj(x, mods, g_pre, w_in, layer, seq_len, cond_base, cond_stride, kv=None, tm=512, tn=1024):
    m, d = x.shape
    n = w_in.shape[-1]
    tm = min(tm, m)
    assert n % tn == 0
    cidx = functools.partial(_cond_index, tm=tm, seq_len=seq_len, cond_base=cond_base, cond_stride=cond_stride)
    in_specs = [
        pl.BlockSpec((tm, d), lambda i: (i, 0)),
        pl.BlockSpec((None, None, 1, N_MOD * d), lambda i: (layer, cidx(i), 0, 0)),
        pl.BlockSpec((None, 1, d), lambda i: (layer, 0, 0)),
        pl.BlockSpec((None, d, n), lambda i: (layer, 0, 0), pipeline_mode=pl.Buffered(1)),
    ]
    n_out = 2 * n // tn
    out_shape = [jax.ShapeDtypeStruct((m, tn // 2), BF16)] * n_out
    out_specs = [pl.BlockSpec((tm, tn // 2), lambda i: (i, 0))] * n_out
    args = [x, mods, g_pre, w_in]
    aliases = {}
    if kv is not None:
        width = kv[0].shape[-1]
        assert tn == 2 * width and tm % seq_len == 0
        kv_spec = pl.BlockSpec((tm // seq_len, None, seq_len, width), lambda i: (i, layer, 0, 0))
        in_specs += [pl.BlockSpec(memory_space=pl.ANY)] * 2
        out_shape += [jax.ShapeDtypeStruct(kv[0].shape, F32)] * 2
        out_specs += [kv_spec, kv_spec]
        args += list(kv)
        aliases = {4: n_out, 5: n_out + 1}
    return pl.pallas_call(
        functools.partial(_in_proj_kernel, d=d, tn=tn, with_kv=kv is not None),
        out_shape=out_shape,
        grid=(m // tm,),
        in_specs=in_specs,
        out_specs=out_specs,
        scratch_shapes=[pltpu.VMEM((tm, d), BF16)],
        input_output_aliases=aliases,
        compiler_params=_params("parallel"),
        name="in_proj",
    )(*args)


def _halo_maps(seq_len, n_rows):
    chunks = seq_len // SEQ_CHUNK
    per = SEQ_CHUNK // HALO
    nblk = n_rows // HALO

    def prev_map(s, c):
        return (jnp.maximum((s * chunks + c) * per - 1, 0), 0)

    def next_map(s, c):
        return (jnp.minimum((s * chunks + c + 1) * per, nblk - 1), 0)

    return prev_map, next_map


def _pool_body(prev_ref, cur_ref, next_ref, w_ref, sc_ref, o_ref, pad_ref, tree_ref, *, chunks, seq_len):
    c = pl.program_id(1)
    cur = cur_ref[...].astype(F32)
    padded = SEQ_CHUNK + 2 * HALO
    pad_ref[0:HALO, :] = jnp.where(c == 0, 0.0, prev_ref[...].astype(F32))
    pad_ref[HALO:HALO + SEQ_CHUNK, :] = cur
    pad_ref[HALO + SEQ_CHUNK:padded, :] = jnp.where(c == chunks - 1, 0.0, next_ref[...].astype(F32))
    pad_ref[padded:padded + HALO, :] = jnp.zeros((HALO, pad_ref.shape[-1]), F32)
    t = c * SEQ_CHUNK + lax.broadcasted_iota(jnp.int32, (SEQ_CHUNK, LANES), 0)
    for gi, w in enumerate(POOL_WINDOWS):
        cols = slice(gi * LANES, (gi + 1) * LANES)
        src, src_cols, span, rows = pad_ref, cols, 1, padded + HALO
        while span < w:
            rows -= SUBLANES
            tree_ref[0:rows, :] = src[0:rows, src_cols] + src[span:span + rows, src_cols]
            src, src_cols, span = tree_ref, slice(None), 2 * span
        acc = src[HALO - w // 2:HALO - w // 2 + SEQ_CHUNK, src_cols]
        lo = jnp.clip(t - w // 2, 0, seq_len)
        hi = jnp.clip(t - w // 2 + w, 0, seq_len)
        pooled = acc / (hi - lo).astype(F32) - cur[:, cols]
        y = jnp.dot(pooled.astype(BF16), w_ref[gi], preferred_element_type=F32) * sc_ref[:, cols]
        o_ref[:, cols] = y.astype(o_ref.dtype)


def _conformer_body(ap_ref, a_ref, an_ref, gp_ref, g_ref, gn_ref, dw_ref, dwb_ref, lng_ref, lnb_ref,
                    pw_ref, pwb_ref, o_ref, pad_ref, sh_ref, *, chunks):
    c = pl.program_id(1)
    glu = lambda a, g: a[...].astype(F32) * _sigmoid(g[...].astype(F32))
    pad_ref[0:HALO, :] = jnp.where(c == 0, 0.0, glu(ap_ref, gp_ref))
    pad_ref[HALO:HALO + SEQ_CHUNK, :] = glu(a_ref, g_ref)
    pad_ref[HALO + SEQ_CHUNK:2 * HALO + SEQ_CHUNK, :] = jnp.where(c == chunks - 1, 0.0, glu(an_ref, gn_ref))
    n_sh = sh_ref.shape[1]
    for r in range(1, SUBLANES):
        sh_ref[r - 1] = pad_ref[r:r + n_sh, :]
    first = HALO - CM_CONV // 2
    acc = None
    for k in range(CM_CONV):
        q, r = divmod(first + k, SUBLANES)
        rows = slice(q * SUBLANES, q * SUBLANES + SEQ_CHUNK)
        tap = dw_ref[k:k + 1, :] * (pad_ref[rows, :] if r == 0 else sh_ref[r - 1, rows, :])
        acc = tap + dwb_ref[...] if acc is None else acc + tap
    mu = jnp.mean(acc, axis=-1, keepdims=True)
    cen = acc - mu
    var = jnp.mean(cen * cen, axis=-1, keepdims=True)
    y = cen * lax.rsqrt(var + EPS) * lng_ref[...] + lnb_ref[...]
    z = (y * _sigmoid(y)).astype(BF16)
    o = jnp.dot(z, pw_ref[...], preferred_element_type=F32) + pwb_ref[...]
    o_ref[...] = o.astype(o_ref.dtype)


def _pool_conformer_kernel(*refs, chunks, seq_len):
    pool_in, cm_in = refs[0:5], refs[5:17]
    pool_out, cm_out, pool_pad, pool_tree, cm_pad, cm_sh = refs[17:23]
    _pool_body(*pool_in, pool_out, pool_pad, pool_tree, chunks=chunks, seq_len=seq_len)
    _conformer_body(*cm_in, cm_out, cm_pad, cm_sh, chunks=chunks)


def pool_conformer(u_pool, u_a, u_g, pool_w, pool_scale, dw_w, dw_b, ln_g, ln_b, pw_w, pw_b, layer, seq_len):
    m = u_pool.shape[0]
    width = dw_w.shape[-1]
    assert width == len(POOL_WINDOWS) * LANES
    n_seq = m // seq_len
    chunks = seq_len // SEQ_CHUNK
    padded = SEQ_CHUNK + 2 * HALO
    vec = lambda: pl.BlockSpec((None, 1, width), lambda s, c: (layer, 0, 0))

    prev_map, next_map = _halo_maps(seq_len, m)
    chunk_specs = [pl.BlockSpec((HALO, width), prev_map),
                   pl.BlockSpec((SEQ_CHUNK, width), lambda s, c: (s * chunks + c, 0)),
                   pl.BlockSpec((HALO, width), next_map)]
    out_spec = pl.BlockSpec((SEQ_CHUNK, width), lambda s, c: (s * chunks + c, 0))
    return pl.pallas_call(
        functools.partial(_pool_conformer_kernel, chunks=chunks, seq_len=seq_len),
        out_shape=(jax.ShapeDtypeStruct((m, width), BF16),) * 2,
        grid=(n_seq, chunks),
        in_specs=chunk_specs + [
            pl.BlockSpec((None, len(POOL_WINDOWS), LANES, LANES), lambda s, c: (layer, 0, 0, 0)),
            vec(),
        ] + chunk_specs + chunk_specs + [
            pl.BlockSpec((None, CM_CONV, width), lambda s, c: (layer, 0, 0)),
            vec(), vec(), vec(),
            pl.BlockSpec((None, width, width), lambda s, c: (layer, 0, 0)),
            vec(),
        ],
        out_specs=(out_spec, out_spec),
        scratch_shapes=[pltpu.VMEM((padded + HALO, width), F32), pltpu.VMEM((padded + HALO, LANES), F32),
                        pltpu.VMEM((padded, width), F32),
                        pltpu.VMEM((SUBLANES - 1, padded - SUBLANES, width), F32)],
        compiler_params=_params("parallel", "parallel"),
        name="pool_conformer",
    )(u_pool, u_pool, u_pool, pool_w, pool_scale, u_a, u_a, u_a, u_g, u_g, u_g, dw_w, dw_b, ln_g, ln_b, pw_w, pw_b)


def _softplus(z):
    return jnp.maximum(z, 0.0) + jnp.log1p(jnp.exp(-jnp.abs(z)))


def _gelu_tanh(x):
    return 0.5 * x * (1.0 + jnp.tanh(0.7978845608028654 * (x + 0.044715 * (x * x * x))))


def _row_select(rows):
    sub = lax.broadcasted_iota(jnp.int32, (SUBLANES, LANES), 0)
    out = jnp.zeros((SUBLANES, LANES), F32)
    for s, r in enumerate(rows):
        out = jnp.where(sub == s, r, out)
    return out


def _lru_kernel(x_ref, g_ref, cw_ref, cb_ref, gw_ref, gb_ref, lam_ref, h0_ref, y_ref, st_ref,
                xpad, a_f, b_f, a_b, b_b, *, seq_len):
    t_len = seq_len
    nb = x_ref.shape[-1] // LANES
    seg = t_len // SUBLANES
    pitch = seg + 4
    width = x_ref.shape[-1]
    xpad[0:SUBLANES, :] = jnp.zeros((SUBLANES, width), F32)
    xpad[SUBLANES + t_len:2 * SUBLANES + t_len, :] = jnp.zeros((SUBLANES, width), F32)
    xpad[SUBLANES:SUBLANES + t_len, :] = x_ref[...].astype(F32)
    left = LRU_CONV // 2
    sp = _softplus(-lam_ref[...])
    for lb in range(nb):
        cols = slice(lb * LANES, (lb + 1) * LANES)
        x = cb_ref[:, cols] + cw_ref[0:1, cols] * xpad[SUBLANES - left:SUBLANES - left + t_len, cols]
        for k in range(1, LRU_CONV):
            x = x + cw_ref[k:k + 1, cols] * xpad[SUBLANES - left + k:SUBLANES - left + k + t_len, cols]
        xb = x.astype(BF16)
        for d, (a_ref, b_ref) in enumerate(((a_f, b_f), (a_b, b_b))):
            gate = lambda gi: _sigmoid(jnp.dot(xb, gw_ref[gi, lb], preferred_element_type=F32) + gb_ref[gi:gi + 1, cols])
            r, i = gate(2 * d), gate(2 * d + 1)
            log_a = (-LRU_C) * r * sp[d:d + 1, cols]
            a = jnp.exp(log_a)
            var = jnp.maximum(-jnp.tanh(log_a) * (a * a + 1.0), 0.0)
            bt = jnp.where(var > 0.0, var * lax.rsqrt(var), 0.0) * (i * x)
            for s in range(SUBLANES):
                a_ref[lb, s * pitch:s * pitch + seg, :] = a[s * seg:(s + 1) * seg, :]
                b_ref[lb, s * pitch:s * pitch + seg, :] = bt[s * seg:(s + 1) * seg, :]

    def rows(ref, lb, j):
        return ref[lb, pl.ds(j, SUBLANES, stride=pitch), :]

    zeros = jnp.zeros((SUBLANES, LANES), F32)
    ones = jnp.ones((SUBLANES, LANES), F32)

    def totals(j, carry):
        out = []
        jb = seg - 1 - j
        for lb in range(nb):
            hf, pf, hb, pb = carry[4 * lb:4 * lb + 4]
            af, ab = rows(a_f, lb, j), rows(a_b, lb, jb)
            out += [af * hf + rows(b_f, lb, j), af * pf, ab * hb + rows(b_b, lb, jb), ab * pb]
        return tuple(out)

    tot = lax.fori_loop(0, seg, totals, (zeros, ones, zeros, ones) * nb)

    cins = []
    for lb in range(nb):
        cols = slice(lb * LANES, (lb + 1) * LANES)
        hf, pf, hb, pb = tot[4 * lb:4 * lb + 4]
        cf = [h0_ref[0:1, cols]]
        for s in range(SUBLANES):
            cf.append(hf[s:s + 1, :] + pf[s:s + 1, :] * cf[-1])
        cb = [h0_ref[1:2, cols]]
        for s in reversed(range(SUBLANES)):
            cb.append(hb[s:s + 1, :] + pb[s:s + 1, :] * cb[-1])
        st_ref[0:1, cols] = cf[SUBLANES]
        st_ref[1:2, cols] = cb[SUBLANES]
        cins += [_row_select(cf[:SUBLANES]), _row_select(list(reversed(cb[:SUBLANES])))]

    def states(j, carry):
        out = []
        jb = seg - 1 - j
        for lb in range(nb):
            hf = rows(a_f, lb, j) * carry[2 * lb] + rows(b_f, lb, j)
            hb = rows(a_b, lb, jb) * carry[2 * lb + 1] + rows(b_b, lb, jb)
            b_f[lb, pl.ds(j, SUBLANES, stride=pitch), :] = hf
            b_b[lb, pl.ds(jb, SUBLANES, stride=pitch), :] = hb
            out += [hf, hb]
        return tuple(out)

    lax.fori_loop(0, seg, states, tuple(cins))

    for lb in range(nb):
        cols = slice(lb * LANES, (lb + 1) * LANES)
        for s in range(SUBLANES):
            h = b_f[lb, s * pitch:s * pitch + seg, :] + b_b[lb, s * pitch:s * pitch + seg, :]
            gate = _gelu_tanh(g_ref[s * seg:(s + 1) * seg, cols].astype(F32))
            y_ref[s * seg:(s + 1) * seg, cols] = (h * gate).astype(y_ref.dtype)


def lru_mixer(u_x, u_g, conv_w, conv_b, gate_w, gate_b, lam, h0, layer, seq_len):
    m = u_x.shape[0]
    width = conv_w.shape[-1]
    nb = width // LANES
    n_seq = m // seq_len
    scan_buf = pltpu.VMEM((nb, SUBLANES * (seq_len // SUBLANES + 4), LANES), F32)
    return pl.pallas_call(
        functools.partial(_lru_kernel, seq_len=seq_len),
        out_shape=(jax.ShapeDtypeStruct((m, width), BF16), jax.ShapeDtypeStruct((n_seq, 2, width), F32)),
        grid=(n_seq,),
        in_specs=[
            pl.BlockSpec((seq_len, width), lambda s: (s, 0)),
            pl.BlockSpec((seq_len, width), lambda s: (s, 0)),
            pl.BlockSpec((None, LRU_CONV, width), lambda s: (layer, 0, 0)),
            pl.BlockSpec((None, 1, width), lambda s: (layer, 0, 0)),
            pl.BlockSpec((None, 4, nb, LANES, LANES), lambda s: (layer, 0, 0, 0, 0)),
            pl.BlockSpec((None, 4, width), lambda s: (layer, 0, 0)),
            pl.BlockSpec((None, 2, width), lambda s: (layer, 0, 0)),
            pl.BlockSpec((None, 2, width), lambda s: (s, 0, 0)),
        ],
        out_specs=(
            pl.BlockSpec((seq_len, width), lambda s: (s, 0)),
            pl.BlockSpec((None, 2, width), lambda s: (s, 0, 0)),
        ),
        scratch_shapes=[pltpu.VMEM((seq_len + 2 * SUBLANES, width), F32), scan_buf, scan_buf, scan_buf, scan_buf],
        compiler_params=_params("parallel"),
        name="lru",
    )(u_x, u_g, conv_w, conv_b, gate_w, gate_b, lam, h0)


def _head_masks():
    lane = lax.broadcasted_iota(jnp.int32, (1, LANES), 1)
    return lane < LANES // 2, lane >= LANES // 2


def _nt_dot(a, b):
    return lax.dot_general(a, b, (((1,), (1,)), ((), ())), preferred_element_type=F32)


def _scaled(q, scale):
    return (q.astype(F32) * scale).astype(BF16)


CTX_SEQS = 4


def _ctx_attn_kernel(q_ref, k_ref, v_ref, o_ref, *, head_dim, seq_len):
    masks = _head_masks()
    zero = jnp.zeros((), BF16)
    for sq in range(q_ref.shape[0] // seq_len):
        rows = slice(sq * seq_len, (sq + 1) * seq_len)
        for p in range(q_ref.shape[-1] // LANES):
            cols = slice(p * LANES, (p + 1) * LANES)
            q = _scaled(q_ref[rows, cols], head_dim ** -0.5)
            k = k_ref[rows, cols]
            v = v_ref[rows, cols]
            out = None
            for msk in masks:
                s = _nt_dot(jnp.where(msk, q, zero), k)
                e = jnp.exp(s - jnp.max(s, axis=-1, keepdims=True))
                den = jnp.sum(e, axis=-1, keepdims=True)
                o = jnp.dot(e.astype(BF16), jnp.where(msk, v, zero), preferred_element_type=F32) / den
                out = o if out is None else out + o
            o_ref[rows, cols] = out.astype(o_ref.dtype)


def context_attention(q, k, v, seq_len, head_dim):
    m, width = q.shape
    n_seq = m // seq_len
    per_step = CTX_SEQS if n_seq % CTX_SEQS == 0 else 1
    spec = pl.BlockSpec((per_step * seq_len, width), lambda s: (s, 0))
    return pl.pallas_call(
        functools.partial(_ctx_attn_kernel, head_dim=head_dim, seq_len=seq_len),
        out_shape=jax.ShapeDtypeStruct((m, width), BF16),
        grid=(n_seq // per_step,),
        in_specs=[spec, spec, spec],
        out_specs=spec,
        compiler_params=_params("parallel"),
        name="ctx_attn",
    )(q, k, v)


def _rpb_tile_kernel(rpb_ref, o_ref):
    n_rows, nd = 2 * NA_KH - 1, 2 * NA_KW - 1
    qc = lax.broadcasted_iota(jnp.int32, (GRID_W, 2 * GRID_W), 0)
    lane = lax.broadcasted_iota(jnp.int32, (GRID_W, 2 * GRID_W), 1)
    kc = lane & (GRID_W - 1)
    start = jnp.clip(qc - NA_KW // 2, 0, GRID_W - NA_KW)
    inside = (kc >= start) & (kc < start + NA_KW)
    idx = jnp.clip(kc - qc + NA_KW - 1, 0, nd - 1)
    base = (pl.program_id(0) * pl.num_programs(1) + pl.program_id(1)) * n_rows
    tiles = []
    for r in range(n_rows):
        acc = jnp.zeros((GRID_W, 2 * GRID_W), F32)
        for dd in range(nd):
            acc = jnp.where(idx == dd, rpb_ref[(base + r) * nd + dd], acc)
        tiles.append(jnp.where(inside, acc, NEG_INF))
    for r in range(n_rows - 1):
        o_ref[r] = jnp.where(lane < GRID_W, tiles[r], tiles[r + 1])


def rpb_tiles(rpb):
    depth, heads, nr, _ = rpb.shape
    return pl.pallas_call(
        _rpb_tile_kernel,
        out_shape=jax.ShapeDtypeStruct((depth, heads, nr - 1, GRID_W, 2 * GRID_W), F32),
        grid=(depth, heads),
        in_specs=[pl.BlockSpec(memory_space=pltpu.SMEM)],
        out_specs=pl.BlockSpec((None, None, nr - 1, GRID_W, 2 * GRID_W), lambda l, h: (l, h, 0, 0, 0)),
        compiler_params=_params("parallel", "parallel"),
        name="rpb_tiles",
    )(rpb.reshape(-1))


NBR_ROWS = 4


def _nbr_attn_kernel(q_ref, k_ref, v_ref, ck_ref, cv_ref, tile_ref, o_ref, *, head_dim, rows, kh):
    masks = _head_masks()
    zero = jnp.zeros((), BF16)
    w = GRID_W
    windows, deltas = [], []
    for rr in range(NBR_ROWS):
        r = pl.program_id(1) * NBR_ROWS + rr
        r0 = jnp.clip(r - kh // 2, 0, rows - kh)
        windows.append(pl.ds(pl.multiple_of(r0 * w, w), kh * w))
        deltas.append(r - r0)
    for p in range(q_ref.shape[-1] // LANES):
        cols = slice(p * LANES, (p + 1) * LANES)
        q = _scaled(q_ref[:, cols], head_dim ** -0.5)
        ck = ck_ref[:, cols]
        cv = cv_ref[:, cols]
        kw = [k_ref[win, cols] for win in windows]
        vw = [v_ref[win, cols] for win in windows]
        out = None
        for hh, msk in enumerate(masks):
            qm = jnp.where(msk, q, zero)
            s_cx = _nt_dot(qm, ck)
            s_nb = []
            for rr in range(NBR_ROWS):
                first = NA_KH - 1 - deltas[rr]
                bias = jnp.concatenate([tile_ref[2 * p + hh, first + 2 * jj] for jj in range(kh // 2)], axis=-1)
                s_nb.append(_nt_dot(qm[rr * w:(rr + 1) * w, :], kw[rr]) + bias)
            s_nb = jnp.concatenate(s_nb, axis=0)
            mx = jnp.maximum(jnp.max(s_nb, axis=-1, keepdims=True), jnp.max(s_cx, axis=-1, keepdims=True))
            e_nb = jnp.exp(s_nb - mx)
            e_cx = jnp.exp(s_cx - mx)
            den = jnp.sum(e_nb, axis=-1, keepdims=True) + jnp.sum(e_cx, axis=-1, keepdims=True)
            e_nb = e_nb.astype(BF16)
            o_nb = jnp.concatenate(
                [jnp.dot(e_nb[rr * w:(rr + 1) * w, :], jnp.where(msk, vw[rr], zero), preferred_element_type=F32)
                 for rr in range(NBR_ROWS)], axis=0)
            o = jnp.dot(e_cx.astype(BF16), jnp.where(msk, cv, zero), preferred_element_type=F32)
            o = (o + o_nb) / den
            out = o if out is None else out + o
        o_ref[:, cols] = out.astype(o_ref.dtype)


def neighbourhood_attention(q, k, v, ctx_k, ctx_v, tiles, layer, seq_len, head_dim):
    m, width = q.shape
    n_seq = m // seq_len
    rows = seq_len // GRID_W
    kh = min(NA_KH, rows)
    assert kh % 2 == 0 and rows % NBR_ROWS == 0
    groups = rows // NBR_ROWS
    past = ctx_k.shape[2]
    return pl.pallas_call(
        functools.partial(_nbr_attn_kernel, head_dim=head_dim, rows=rows, kh=kh),
        out_shape=jax.ShapeDtypeStruct((m, width), BF16),
        grid=(n_seq, groups),
        in_specs=[
            pl.BlockSpec((NBR_ROWS * GRID_W, width), lambda b, g: (b * groups + g, 0)),
            pl.BlockSpec((seq_len, width), lambda b, g: (b, 0)),
            pl.BlockSpec((seq_len, width), lambda b, g: (b, 0)),
            pl.BlockSpec((None, None, past, width), lambda b, g: (b, layer, 0, 0)),
            pl.BlockSpec((None, None, past, width), lambda b, g: (b, layer, 0, 0)),
            pl.BlockSpec((None,) + tiles.shape[1:], lambda b, g: (layer, 0, 0, 0, 0)),
        ],
        out_specs=pl.BlockSpec((NBR_ROWS * GRID_W, width), lambda b, g: (b * groups + g, 0)),
        compiler_params=_params("parallel", "parallel"),
        name="nbr_attn",
    )(q, k, v, ctx_k, ctx_v, tiles)


def _out_proj_kernel(y0_ref, y1_ref, y2_ref, y3_ref, w_ref, x_ref, mod_ref, g_ref, o_ref, *, d, gw):
    gg = g_ref[...] * mod_ref[:, 2 * d:3 * d]
    for r in range(x_ref.shape[0] // ROW_CHUNK):
        rows = slice(r * ROW_CHUNK, (r + 1) * ROW_CHUNK)
        mix = None
        for gi, y_ref in enumerate((y0_ref, y1_ref, y2_ref, y3_ref)):
            part = jnp.dot(y_ref[rows, :], w_ref[gi * gw:(gi + 1) * gw, :], preferred_element_type=F32)
            mix = part if mix is None else mix + part
        rs = lax.rsqrt(jnp.mean(mix * mix, axis=-1, keepdims=True) + EPS)
        o_ref[rows, :] = x_ref[rows, :] + mix * rs * gg


def out_proj(ys, w_out, x, mods, g_post, layer, seq_len, cond_base, cond_stride, tm=512):
    m, d = x.shape
    gw = ys[0].shape[-1]
    tm = min(tm, m)
    cidx = functools.partial(_cond_index, tm=tm, seq_len=seq_len, cond_base=cond_base, cond_stride=cond_stride)
    yspec = pl.BlockSpec((tm, gw), lambda i: (i, 0))
    return pl.pallas_call(
        functools.partial(_out_proj_kernel, d=d, gw=gw),
        out_shape=jax.ShapeDtypeStruct((m, d), F32),
        grid=(m // tm,),
        in_specs=[
            yspec, yspec, yspec, yspec,
            pl.BlockSpec((None, w_out.shape[1], d), lambda i: (layer, 0, 0), pipeline_mode=pl.Buffered(1)),
            pl.BlockSpec((tm, d), lambda i: (i, 0)),
            pl.BlockSpec((None, None, 1, N_MOD * d), lambda i: (layer, cidx(i), 0, 0)),
            pl.BlockSpec((None, 1, d), lambda i: (layer, 0, 0)),
        ],
        out_specs=pl.BlockSpec((tm, d), lambda i: (i, 0)),
        compiler_params=_params("parallel"),
        name="out_proj",
    )(*ys, w_out, x, mods, g_post)


def _ffn_kernel(x_ref, mod_ref, gpre_ref, gpost_ref, w1_ref, w3_ref, w2_ref, o_ref, h_ref, *, d, rb):
    j = pl.program_id(1)

    @pl.when(j == 0)
    def _():
        _norm_modulate(x_ref, gpre_ref[...], mod_ref[:, 3 * d:4 * d], mod_ref[:, 4 * d:5 * d], h_ref)

    def step(first):
        w1, w3, w2 = w1_ref[...], w3_ref[...], w2_ref[...].astype(BF16)
        for r in range(h_ref.shape[0] // rb):
            rows = slice(r * rb, (r + 1) * rb)
            h = h_ref[rows, :]
            a = jnp.dot(h, w1, preferred_element_type=F32)
            b = jnp.dot(h, w3, preferred_element_type=F32)
            hid = (a * _sigmoid(a) * b).astype(BF16)
            part = jnp.dot(hid, w2, preferred_element_type=F32)
            if first:
                o_ref[rows, :] = part
            else:
                o_ref[rows, :] += part

    last = pl.num_programs(1) - 1

    @pl.when(j == 0)
    def _():
        step(True)

    @pl.when((j > 0) & (j < last))
    def _():
        step(False)

    @pl.when(j == last)
    def _():
        step(False)
        _gated_residual(x_ref, o_ref, gpost_ref[...], mod_ref[:, 5 * d:6 * d], o_ref)


def ffn(x, mods, g_pre, g_post, w1, w3, w2, layer, seq_len, cond_base, cond_stride, tm=1024, tf=512, rb=512):
    m, d = x.shape
    f = w1.shape[-1]
    tm = min(tm, m)
    rb = min(rb, tm)
    cidx = functools.partial(_cond_index, tm=tm, seq_len=seq_len, cond_base=cond_base, cond_stride=cond_stride)
    return pl.pallas_call(
        functools.partial(_ffn_kernel, d=d, rb=rb),
        out_shape=jax.ShapeDtypeStruct((m, d), F32),
        grid=(m // tm, f // tf),
        in_specs=[
            pl.BlockSpec((tm, d), lambda i, j: (i, 0)),
            pl.BlockSpec((None, None, 1, N_MOD * d), lambda i, j: (layer, cidx(i), 0, 0)),
            pl.BlockSpec((None, 1, d), lambda i, j: (layer, 0, 0)),
            pl.BlockSpec((None, 1, d), lambda i, j: (layer, 0, 0)),
            pl.BlockSpec((None, d, tf), lambda i, j: (layer, 0, j)),
            pl.BlockSpec((None, d, tf), lambda i, j: (layer, 0, j)),
            pl.BlockSpec((None, tf, d), lambda i, j: (layer, j, 0)),
        ],
        out_specs=pl.BlockSpec((tm, d), lambda i, j: (i, 0)),
        scratch_shapes=[pltpu.VMEM((tm, d), BF16)],
        compiler_params=_params("parallel", "arbitrary"),
        name="ffn",
    )(x, mods, g_pre, g_post, w1, w3, w2)


def _block_diag_gates(gate_w):
    depth, _, _, nblk, c, _ = gate_w.shape
    per = LANES // c
    w = gate_w.reshape(depth, 4, nblk // per, per, c, c).astype(BF16)
    eye = jnp.eye(per, dtype=BF16)
    tiles = jnp.einsum('lgnpce,pq->lgnpcqe', w, eye)
    return tiles.reshape(depth, 4, nblk // per, LANES, LANES)


def kernel(x_prompt, x_sample, cache_k, cache_v, state_lru, c, c_ctx, w_ada, b_ada, g_pre_mix, g_post_mix,
           g_pre_ffn, g_post_ffn, w_in, pool_w, pool_scale, lru_conv_w, lru_conv_b, lru_gate_w, lru_gate_b,
           lru_lambda, cm_dw_w, cm_dw_b, cm_ln_g, cm_ln_b, cm_pw_w, cm_pw_b, na_rpb, w_out, ffn_w1, ffn_w3, ffn_w2):
    batch, seq, d = x_prompt.shape
    dec_batch, dec_seq, _ = x_sample.shape
    depth = w_in.shape[0]
    gw = pool_scale.shape[-1]
    heads, head_dim = cache_k.shape[-2], cache_k.shape[-1]
    past = cache_k.shape[2]
    assert dec_batch + 1 <= 8 and seq % SEQ_CHUNK == 0 and dec_seq % SEQ_CHUNK == 0

    cond = jnp.zeros((8, d), F32).at[0].set(c_ctx).at[1:1 + dec_batch].set(c)
    mods = adaln_all(cond, w_ada, b_ada).reshape(depth, 8, 1, N_MOD * d)

    vec = lambda a: a.reshape(depth, 1, a.shape[-1])
    g_pre_mix, g_post_mix, g_pre_ffn, g_post_ffn = map(vec, (g_pre_mix, g_post_mix, g_pre_ffn, g_post_ffn))
    pool_scale, lru_conv_b, cm_dw_b, cm_ln_g, cm_ln_b, cm_pw_b = map(
        vec, (pool_scale, lru_conv_b, cm_dw_b, cm_ln_g, cm_ln_b, cm_pw_b))
    w_in_b, w_out_b = w_in.astype(BF16), w_out.astype(BF16)
    w1_b, w3_b = ffn_w1.astype(BF16), ffn_w3.astype(BF16)
    pool_w_b, cm_pw_b16 = pool_w.astype(BF16), cm_pw_w.astype(BF16)
    gate_tiles = _block_diag_gates(lru_gate_w)
    gate_b = lru_gate_b.reshape(depth, 4, gw)
    tiles = rpb_tiles(na_rpb)
    ctx_k = cache_k.reshape(dec_batch, depth, past, gw).astype(BF16)
    ctx_v = cache_v.reshape(dec_batch, depth, past, gw).astype(BF16)

    def layer_step(x, l, seq_len, cond_base, cond_stride, h0, kv):
        res = in_proj(x, mods, g_pre_mix, w_in_b, l, seq_len, cond_base, cond_stride, kv=kv)
        u_pool, u_lru_x, u_lru_g, u_cm_a, u_cm_g, u_q, u_k, u_v = res[:8]
        y_pool, y_cm = pool_conformer(u_pool, u_cm_a, u_cm_g, pool_w_b, pool_scale, cm_dw_w, cm_dw_b, cm_ln_g, cm_ln_b,
                                      cm_pw_b16, cm_pw_b, l, seq_len)
        y_lru, st = lru_mixer(u_lru_x, u_lru_g, lru_conv_w, lru_conv_b, gate_tiles, gate_b, lru_lambda, h0, l, seq_len)
        if kv is None:
            y_att = neighbourhood_attention(u_q, u_k, u_v, ctx_k, ctx_v, tiles, l, seq_len, head_dim)
        else:
            y_att = context_attention(u_q, u_k, u_v, seq_len, head_dim)
        x = out_proj((y_pool, y_lru, y_cm, y_att), w_out_b, x, mods, g_post_mix, l, seq_len, cond_base, cond_stride)
        x = ffn(x, mods, g_pre_ffn, g_post_ffn, w1_b, w3_b, ffn_w2, l, seq_len, cond_base, cond_stride)
        return x, st, tuple(res[8:])

    xp = x_prompt.reshape(batch * seq, d)
    xs = x_sample.reshape(dec_batch * dec_seq, d)
    zeros_h0 = jnp.zeros((batch, 2, gw), F32)
    kv = tuple(jnp.zeros((batch, depth, seq, gw), F32) for _ in range(2))
    sts = []
    for l in range(depth):
        xp, st, kv = layer_step(xp, l, seq, 0, 0, zeros_h0, kv)
        sts.append(st)
        xs, _, _ = layer_step(xs, l, dec_seq, 1, 1, state_lru[:, l], None)

    out_kv = (batch, depth, seq, heads, head_dim)
    return (xp.reshape(batch, seq, d), xs.reshape(dec_batch, dec_seq, d),
            kv[0].reshape(out_kv), kv[1].reshape(out_kv), jnp.stack(sts, axis=1))
```

```python
import functools

import jax
import jax.numpy as jnp
from jax import lax
from jax.experimental import pallas as pl
from jax.experimental.pallas import tpu as pltpu

F32 = jnp.float32
BF16 = jnp.bfloat16

EPS = 1e-6
NEG_INF = -1e30
N_MOD = 6
POOL_WINDOWS = (2, 4, 8, 16)
LRU_CONV = 4
LRU_C = 8.0
CM_CONV = 31
NA_KH = 8
NA_KW = 16
GRID_W = 64

LANES = 128
SUBLANES = 8
VMEM_LIMIT_BYTES = 60 * 1024 * 1024
SEQ_CHUNK = 256
HALO = 16


def _params(*sem):
    return pltpu.CompilerParams(dimension_semantics=sem, vmem_limit_bytes=VMEM_LIMIT_BYTES)


def _sigmoid(x):
    return 0.5 * jnp.tanh(0.5 * x) + 0.5


def _adaln_kernel(cond_ref, w_ref, b_ref, o_ref):
    c = cond_ref[...]
    s = (c * _sigmoid(c)).astype(BF16)
    o_ref[...] = jnp.dot(s, w_ref[...].astype(BF16), preferred_element_type=F32) + b_ref[...]


def adaln_all(cond, w_ada, b_ada, tn=2048):
    depth, d, n = w_ada.shape
    return pl.pallas_call(
        _adaln_kernel,
        out_shape=jax.ShapeDtypeStruct((depth, 8, n), F32),
        grid=(depth, n // tn),
        in_specs=[
            pl.BlockSpec((8, d), lambda l, j: (0, 0)),
            pl.BlockSpec((None, d, tn), lambda l, j: (l, 0, j)),
            pl.BlockSpec((None, 1, tn), lambda l, j: (l, 0, j)),
        ],
        out_specs=pl.BlockSpec((None, 8, tn), lambda l, j: (l, 0, j)),
        compiler_params=_params("parallel", "parallel"),
        name="adaln",
    )(cond, w_ada, b_ada.reshape(depth, 1, n))


ROW_CHUNK = 128


def _norm_modulate(x_ref, g, shift, scale, h_ref):
    gs = g * (1.0 + scale)
    for r in range(x_ref.shape[0] // ROW_CHUNK):
        rows = slice(r * ROW_CHUNK, (r + 1) * ROW_CHUNK)
        x = x_ref[rows, :]
        rs = lax.rsqrt(jnp.mean(x * x, axis=-1, keepdims=True) + EPS)
        h_ref[rows, :] = (x * rs * gs + shift).astype(h_ref.dtype)


def _gated_residual(x_ref, y_ref, g, gate, o_ref):
    gg = g * gate
    for r in range(x_ref.shape[0] // ROW_CHUNK):
        rows = slice(r * ROW_CHUNK, (r + 1) * ROW_CHUNK)
        y = y_ref[rows, :]
        rs = lax.rsqrt(jnp.mean(y * y, axis=-1, keepdims=True) + EPS)
        o_ref[rows, :] = x_ref[rows, :] + y * rs * gg


def _in_proj_kernel(x_ref, mod_ref, g_ref, w_ref, *rest, d, tn, with_kv):
    if with_kv:
        rest = rest[2:]
        k_ref, v_ref = rest[-3:-1]
    h_ref = rest[-1]
    n = w_ref.shape[-1]
    half = tn // 2
    u_refs = rest[:n // half]
    _norm_modulate(x_ref, g_ref[...], mod_ref[:, 0:d], mod_ref[:, d:2 * d], h_ref)
    for c in range(n // tn):
        res = jnp.dot(h_ref[...], w_ref[:, c * tn:(c + 1) * tn], preferred_element_type=F32)
        u_refs[2 * c][...] = res[:, :half].astype(BF16)
        u_refs[2 * c + 1][...] = res[:, half:].astype(BF16)
        if with_kv and c == n // tn - 1:
            k_ref[...] = res[:, :half].reshape(k_ref.shape)
            v_ref[...] = res[:, half:].reshape(v_ref.shape)


def _cond_index(i, tm, seq_len, cond_base, cond_stride):
    assert cond_stride == 0 or seq_len % tm == 0, "a row tile must not straddle two conditioning vectors"
    return cond_base + ((i * tm) // seq_len) * cond_stride


def in_proj(x, mods, g_pre, w_in, layer, seq_len, cond_base, cond_stride, kv=None, tm=512, tn=1024):
    m, d = x.shape
    n = w_in.shape[-1]
    tm = min(tm, m)
    assert n % tn == 0
    cidx = functools.partial(_cond_index, tm=tm, seq_len=seq_len, cond_base=cond_base, cond_stride=cond_stride)
    in_specs = [
        pl.BlockSpec((tm, d), lambda i: (i, 0)),
        pl.BlockSpec((None, None, 1, N_MOD * d), lambda i: (layer, cidx(i), 0, 0)),
        pl.BlockSpec((None, 1, d), lambda i: (layer, 0, 0)),
        pl.BlockSpec((None, d, n), lambda i: (layer, 0, 0), pipeline_mode=pl.Buffered(1)),
    ]
    n_out = 2 * n // tn
    out_shape = [jax.ShapeDtypeStruct((m, tn // 2), BF16)] * n_out
    out_specs = [pl.BlockSpec((tm, tn // 2), lambda i: (i, 0))] * n_out
    args = [x, mods, g_pre, w_in]
    aliases = {}
    if kv is not None:
        width = kv[0].shape[-1]
        assert tn == 2 * width and tm % seq_len == 0
        kv_spec = pl.BlockSpec((tm // seq_len, None, seq_len, width), lambda i: (i, layer, 0, 0))
        in_specs += [pl.BlockSpec(memory_space=pl.ANY)] * 2
        out_shape += [jax.ShapeDtypeStruct(kv[0].shape, F32)] * 2
        out_specs += [kv_spec, kv_spec]
        args += list(kv)
        aliases = {4: n_out, 5: n_out + 1}
    return pl.pallas_call(
        functools.partial(_in_proj_kernel, d=d, tn=tn, with_kv=kv is not None),
        out_shape=out_shape,
        grid=(m // tm,),
        in_specs=in_specs,
        out_specs=out_specs,
        scratch_shapes=[pltpu.VMEM((tm, d), BF16)],
        input_output_aliases=aliases,
        compiler_params=_params("parallel"),
        name="in_proj",
    )(*args)


def _halo_maps(seq_len, n_rows):
    chunks = seq_len // SEQ_CHUNK
    per = SEQ_CHUNK // HALO
    nblk = n_rows // HALO

    def prev_map(s, c):
        return (jnp.maximum((s * chunks + c) * per - 1, 0), 0)

    def next_map(s, c):
        return (jnp.minimum((s * chunks + c + 1) * per, nblk - 1), 0)

    return prev_map, next_map


def _pool_body(prev_ref, cur_ref, next_ref, w_ref, sc_ref, o_ref, pad_ref, tree_ref, *, chunks, seq_len):
    c = pl.program_id(1)
    cur = cur_ref[...].astype(F32)
    padded = SEQ_CHUNK + 2 * HALO
    pad_ref[0:HALO, :] = jnp.where(c == 0, 0.0, prev_ref[...].astype(F32))
    pad_ref[HALO:HALO + SEQ_CHUNK, :] = cur
    pad_ref[HALO + SEQ_CHUNK:padded, :] = jnp.where(c == chunks - 1, 0.0, next_ref[...].astype(F32))
    pad_ref[padded:padded + HALO, :] = jnp.zeros((HALO, pad_ref.shape[-1]), F32)
    t = c * SEQ_CHUNK + lax.broadcasted_iota(jnp.int32, (SEQ_CHUNK, LANES), 0)
    for gi, w in enumerate(POOL_WINDOWS):
        cols = slice(gi * LANES, (gi + 1) * LANES)
        src, src_cols, span, rows = pad_ref, cols, 1, padded + HALO
        while span < w:
            rows -= SUBLANES
            tree_ref[0:rows, :] = src[0:rows, src_cols] + src[span:span + rows, src_cols]
            src, src_cols, span = tree_ref, slice(None), 2 * span
        acc = src[HALO - w // 2:HALO - w // 2 + SEQ_CHUNK, src_cols]
        lo = jnp.clip(t - w // 2, 0, seq_len)
        hi = jnp.clip(t - w // 2 + w, 0, seq_len)
        pooled = acc / (hi - lo).astype(F32) - cur[:, cols]
        y = jnp.dot(pooled.astype(BF16), w_ref[gi], preferred_element_type=F32) * sc_ref[:, cols]
        o_ref[:, cols] = y.astype(o_ref.dtype)


def _conformer_body(ap_ref, a_ref, an_ref, gp_ref, g_ref, gn_ref, dw_ref, dwb_ref, lng_ref, lnb_ref,
                    pw_ref, pwb_ref, o_ref, pad_ref, sh_ref, *, chunks):
    c = pl.program_id(1)
    glu = lambda a, g: a[...].astype(F32) * _sigmoid(g[...].astype(F32))
    pad_ref[0:HALO, :] = jnp.where(c == 0, 0.0, glu(ap_ref, gp_ref))
    pad_ref[HALO:HALO + SEQ_CHUNK, :] = glu(a_ref, g_ref)
    pad_ref[HALO + SEQ_CHUNK:2 * HALO + SEQ_CHUNK, :] = jnp.where(c == chunks - 1, 0.0, glu(an_ref, gn_ref))
    n_sh = sh_ref.shape[1]
    for r in range(1, SUBLANES):
        sh_ref[r - 1] = pad_ref[r:r + n_sh, :]
    first = HALO - CM_CONV // 2
    acc = None
    for k in range(CM_CONV):
        q, r = divmod(first + k, SUBLANES)
        rows = slice(q * SUBLANES, q * SUBLANES + SEQ_CHUNK)
        tap = dw_ref[k:k + 1, :] * (pad_ref[rows, :] if r == 0 else sh_ref[r - 1, rows, :])
        acc = tap + dwb_ref[...] if acc is None else acc + tap
    mu = jnp.mean(acc, axis=-1, keepdims=True)
    cen = acc - mu
    var = jnp.mean(cen * cen, axis=-1, keepdims=True)
    y = cen * lax.rsqrt(var + EPS) * lng_ref[...] + lnb_ref[...]
    z = (y * _sigmoid(y)).astype(BF16)
    o = jnp.dot(z, pw_ref[...], preferred_element_type=F32) + pwb_ref[...]
    o_ref[...] = o.astype(o_ref.dtype)


def _pool_conformer_kernel(*refs, chunks, seq_len):
    pool_in, cm_in = refs[0:5], refs[5:17]
    pool_out, cm_out, pool_pad, pool_tree, cm_pad, cm_sh = refs[17:23]
    _pool_body(*pool_in, pool_out, pool_pad, pool_tree, chunks=chunks, seq_len=seq_len)
    _conformer_body(*cm_in, cm_out, cm_pad, cm_sh, chunks=chunks)


def pool_conformer(u_pool, u_a, u_g, pool_w, pool_scale, dw_w, dw_b, ln_g, ln_b, pw_w, pw_b, layer, seq_len):
    m = u_pool.shape[0]
    width = dw_w.shape[-1]
    assert width == len(POOL_WINDOWS) * LANES
    n_seq = m // seq_len
    chunks = seq_len // SEQ_CHUNK
    padded = SEQ_CHUNK + 2 * HALO
    vec = lambda: pl.BlockSpec((None, 1, width), lambda s, c: (layer, 0, 0))

    prev_map, next_map = _halo_maps(seq_len, m)
    chunk_specs = [pl.BlockSpec((HALO, width), prev_map),
                   pl.BlockSpec((SEQ_CHUNK, width), lambda s, c: (s * chunks + c, 0)),
                   pl.BlockSpec((HALO, width), next_map)]
    out_spec = pl.BlockSpec((SEQ_CHUNK, width), lambda s, c: (s * chunks + c, 0))
    return pl.pallas_call(
        functools.partial(_pool_conformer_kernel, chunks=chunks, seq_len=seq_len),
        out_shape=(jax.ShapeDtypeStruct((m, width), BF16),) * 2,
        grid=(n_seq, chunks),
        in_specs=chunk_specs + [
            pl.BlockSpec((None, len(POOL_WINDOWS), LANES, LANES), lambda s, c: (layer, 0, 0, 0)),
            vec(),
        ] + chunk_specs + chunk_specs + [
            pl.BlockSpec((None, CM_CONV, width), lambda s, c: (layer, 0, 0)),
            vec(), vec(), vec(),
            pl.BlockSpec((None, width, width), lambda s, c: (layer, 0, 0)),
            vec(),
        ],
        out_specs=(out_spec, out_spec),
        scratch_shapes=[pltpu.VMEM((padded + HALO, width), F32), pltpu.VMEM((padded + HALO, LANES), F32),
                        pltpu.VMEM((padded, width), F32),
                        pltpu.VMEM((SUBLANES - 1, padded - SUBLANES, width), F32)],
        compiler_params=_params("parallel", "parallel"),
        name="pool_conformer",
    )(u_pool, u_pool, u_pool, pool_w, pool_scale, u_a, u_a, u_a, u_g, u_g, u_g, dw_w, dw_b, ln_g, ln_b, pw_w, pw_b)


def _softplus(z):
    return jnp.maximum(z, 0.0) + jnp.log1p(jnp.exp(-jnp.abs(z)))


def _gelu_tanh(x):
    return 0.5 * x * (1.0 + jnp.tanh(0.7978845608028654 * (x + 0.044715 * (x * x * x))))


def _row_select(rows):
    sub = lax.broadcasted_iota(jnp.int32, (SUBLANES, LANES), 0)
    out = jnp.zeros((SUBLANES, LANES), F32)
    for s, r in enumerate(rows):
        out = jnp.where(sub == s, r, out)
    return out


def _lru_kernel(x_ref, g_ref, cw_ref, cb_ref, gw_ref, gb_ref, lam_ref, h0_ref, y_ref, st_ref,
                xpad, a_f, b_f, a_b, b_b, *, seq_len):
    t_len = seq_len
    nb = x_ref.shape[-1] // LANES
    seg = t_len // SUBLANES
    pitch = seg + 4
    width = x_ref.shape[-1]
    xpad[0:SUBLANES, :] = jnp.zeros((SUBLANES, width), F32)
    xpad[SUBLANES + t_len:2 * SUBLANES + t_len, :] = jnp.zeros((SUBLANES, width), F32)
    xpad[SUBLANES:SUBLANES + t_len, :] = x_ref[...].astype(F32)
    left = LRU_CONV // 2
    sp = _softplus(-lam_ref[...])
    for lb in range(nb):
        cols = slice(lb * LANES, (lb + 1) * LANES)
        x = cb_ref[:, cols] + cw_ref[0:1, cols] * xpad[SUBLANES - left:SUBLANES - left + t_len, cols]
        for k in range(1, LRU_CONV):
            x = x + cw_ref[k:k + 1, cols] * xpad[SUBLANES - left + k:SUBLANES - left + k + t_len, cols]
        xb = x.astype(BF16)
        for d, (a_ref, b_ref) in enumerate(((a_f, b_f), (a_b, b_b))):
            gate = lambda gi: _sigmoid(jnp.dot(xb, gw_ref[gi, lb], preferred_element_type=F32) + gb_ref[gi:gi + 1, cols])
            r, i = gate(2 * d), gate(2 * d + 1)
            neg_log_a = LRU_C * r * sp[d:d + 1, cols]
            a = jnp.exp(-neg_log_a)
            var = jnp.maximum(jnp.tanh(neg_log_a) * (a * a + 1.0), 0.0)
            bt = jnp.where(var > 0.0, var * lax.rsqrt(var), 0.0) * (i * x)
            for s in range(SUBLANES):
                a_ref[lb, s * pitch:s * pitch + seg, :] = a[s * seg:(s + 1) * seg, :]
                b_ref[lb, s * pitch:s * pitch + seg, :] = bt[s * seg:(s + 1) * seg, :]

    def rows(ref, lb, j):
        return ref[lb, pl.ds(j, SUBLANES, stride=pitch), :]

    zeros = jnp.zeros((SUBLANES, LANES), F32)
    ones = jnp.ones((SUBLANES, LANES), F32)

    def totals(j, carry):
        out = []
        jb = seg - 1 - j
        for lb in range(nb):
            hf, pf, hb, pb = carry[4 * lb:4 * lb + 4]
            af, ab = rows(a_f, lb, j), rows(a_b, lb, jb)
            out += [af * hf + rows(b_f, lb, j), af * pf, ab * hb + rows(b_b, lb, jb), ab * pb]
        return tuple(out)

    tot = lax.fori_loop(0, seg, totals, (zeros, ones, zeros, ones) * nb)

    cins = []
    for lb in range(nb):
        cols = slice(lb * LANES, (lb + 1) * LANES)
        hf, pf, hb, pb = tot[4 * lb:4 * lb + 4]
        cf = [h0_ref[0:1, cols]]
        for s in range(SUBLANES):
            cf.append(hf[s:s + 1, :] + pf[s:s + 1, :] * cf[-1])
        cb = [h0_ref[1:2, cols]]
        for s in reversed(range(SUBLANES)):
            cb.append(hb[s:s + 1, :] + pb[s:s + 1, :] * cb[-1])
        st_ref[0:1, cols] = cf[SUBLANES]
        st_ref[1:2, cols] = cb[SUBLANES]
        cins += [_row_select(cf[:SUBLANES]), _row_select(list(reversed(cb[:SUBLANES])))]

    def states(j, carry):
        out = []
        jb = seg - 1 - j
        for lb in range(nb):
            hf = rows(a_f, lb, j) * carry[2 * lb] + rows(b_f, lb, j)
            hb = rows(a_b, lb, jb) * carry[2 * lb + 1] + rows(b_b, lb, jb)
            b_f[lb, pl.ds(j, SUBLANES, stride=pitch), :] = hf
            b_b[lb, pl.ds(jb, SUBLANES, stride=pitch), :] = hb
            out += [hf, hb]
        return tuple(out)

    lax.fori_loop(0, seg, states, tuple(cins))

    for lb in range(nb):
        cols = slice(lb * LANES, (lb + 1) * LANES)
        for s in range(SUBLANES):
            h = b_f[lb, s * pitch:s * pitch + seg, :] + b_b[lb, s * pitch:s * pitch + seg, :]
            gate = _gelu_tanh(g_ref[s * seg:(s + 1) * seg, cols].astype(F32))
            y_ref[s * seg:(s + 1) * seg, cols] = (h * gate).astype(y_ref.dtype)


def lru_mixer(u_x, u_g, conv_w, conv_b, gate_w, gate_b, lam, h0, layer, seq_len):
    m = u_x.shape[0]
    width = conv_w.shape[-1]
    nb = width // LANES
    n_seq = m // seq_len
    scan_buf = pltpu.VMEM((nb, SUBLANES * (seq_len // SUBLANES + 4), LANES), F32)
    return pl.pallas_call(
        functools.partial(_lru_kernel, seq_len=seq_len),
        out_shape=(jax.ShapeDtypeStruct((m, width), BF16), jax.ShapeDtypeStruct((n_seq, 2, width), F32)),
        grid=(n_seq,),
        in_specs=[
            pl.BlockSpec((seq_len, width), lambda s: (s, 0)),
            pl.BlockSpec((seq_len, width), lambda s: (s, 0)),
            pl.BlockSpec((None, LRU_CONV, width), lambda s: (layer, 0, 0)),
            pl.BlockSpec((None, 1, width), lambda s: (layer, 0, 0)),
            pl.BlockSpec((None, 4, nb, LANES, LANES), lambda s: (layer, 0, 0, 0, 0)),
            pl.BlockSpec((None, 4, width), lambda s: (layer, 0, 0)),
            pl.BlockSpec((None, 2, width), lambda s: (layer, 0, 0)),
            pl.BlockSpec((None, 2, width), lambda s: (s, 0, 0)),
        ],
        out_specs=(
            pl.BlockSpec((seq_len, width), lambda s: (s, 0)),
            pl.BlockSpec((None, 2, width), lambda s: (s, 0, 0)),
        ),
        scratch_shapes=[pltpu.VMEM((seq_len + 2 * SUBLANES, width), F32), scan_buf, scan_buf, scan_buf, scan_buf],
        compiler_params=_params("parallel"),
        name="lru",
    )(u_x, u_g, conv_w, conv_b, gate_w, gate_b, lam, h0)


def _head_masks():
    lane = lax.broadcasted_iota(jnp.int32, (1, LANES), 1)
    return lane < LANES // 2, lane >= LANES // 2


def _nt_dot(a, b):
    return lax.dot_general(a, b, (((1,), (1,)), ((), ())), preferred_element_type=F32)


def _scaled(q, scale):
    return (q.astype(F32) * scale).astype(BF16)


CTX_SEQS = 4


def _ctx_attn_kernel(q_ref, k_ref, v_ref, o_ref, *, head_dim, seq_len):
    masks = _head_masks()
    zero = jnp.zeros((), BF16)
    for sq in range(q_ref.shape[0] // seq_len):
        rows = slice(sq * seq_len, (sq + 1) * seq_len)
        for p in range(q_ref.shape[-1] // LANES):
            cols = slice(p * LANES, (p + 1) * LANES)
            q = _scaled(q_ref[rows, cols], head_dim ** -0.5)
            k = k_ref[rows, cols]
            v = v_ref[rows, cols]
            out = None
            for msk in masks:
                s = _nt_dot(jnp.where(msk, q, zero), k)
                e = jnp.exp(s - jnp.max(s, axis=-1, keepdims=True))
                den = jnp.sum(e, axis=-1, keepdims=True)
                o = jnp.dot(e.astype(BF16), jnp.where(msk, v, zero), preferred_element_type=F32) / den
                out = o if out is None else out + o
            o_ref[rows, cols] = out.astype(o_ref.dtype)


def context_attention(q, k, v, seq_len, head_dim):
    m, width = q.shape
    n_seq = m // seq_len
    per_step = CTX_SEQS if n_seq % CTX_SEQS == 0 else 1
    spec = pl.BlockSpec((per_step * seq_len, width), lambda s: (s, 0))
    return pl.pallas_call(
        functools.partial(_ctx_attn_kernel, head_dim=head_dim, seq_len=seq_len),
        out_shape=jax.ShapeDtypeStruct((m, width), BF16),
        grid=(n_seq // per_step,),
        in_specs=[spec, spec, spec],
        out_specs=spec,
        compiler_params=_params("parallel"),
        name="ctx_attn",
    )(q, k, v)


def _rpb_tile_kernel(rpb_ref, o_ref):
    n_rows, nd = 2 * NA_KH - 1, 2 * NA_KW - 1
    qc = lax.broadcasted_iota(jnp.int32, (GRID_W, 2 * GRID_W), 0)
    lane = lax.broadcasted_iota(jnp.int32, (GRID_W, 2 * GRID_W), 1)
    kc = lane & (GRID_W - 1)
    start = jnp.clip(qc - NA_KW // 2, 0, GRID_W - NA_KW)
    inside = (kc >= start) & (kc < start + NA_KW)
    idx = jnp.clip(kc - qc + NA_KW - 1, 0, nd - 1)
    base = (pl.program_id(0) * pl.num_programs(1) + pl.program_id(1)) * n_rows
    tiles = []
    for r in range(n_rows):
        acc = jnp.zeros((GRID_W, 2 * GRID_W), F32)
        for dd in range(nd):
            acc = jnp.where(idx == dd, rpb_ref[(base + r) * nd + dd], acc)
        tiles.append(jnp.where(inside, acc, NEG_INF))
    for r in range(n_rows - 1):
        o_ref[r] = jnp.where(lane < GRID_W, tiles[r], tiles[r + 1])


def rpb_tiles(rpb):
    depth, heads, nr, _ = rpb.shape
    return pl.pallas_call(
        _rpb_tile_kernel,
        out_shape=jax.ShapeDtypeStruct((depth, heads, nr - 1, GRID_W, 2 * GRID_W), F32),
        grid=(depth, heads),
        in_specs=[pl.BlockSpec(memory_space=pltpu.SMEM)],
        out_specs=pl.BlockSpec((None, None, nr - 1, GRID_W, 2 * GRID_W), lambda l, h: (l, h, 0, 0, 0)),
        compiler_params=_params("parallel", "parallel"),
        name="rpb_tiles",
    )(rpb.reshape(-1))


NBR_ROWS = 8


def _nbr_attn_kernel(q_ref, k_ref, v_ref, ck_ref, cv_ref, tile_ref, o_ref, *, head_dim, rows, kh):
    masks = _head_masks()
    zero = jnp.zeros((), BF16)
    w = GRID_W
    windows, deltas = [], []
    for rr in range(NBR_ROWS):
        r = pl.program_id(1) * NBR_ROWS + rr
        r0 = jnp.clip(r - kh // 2, 0, rows - kh)
        windows.append(pl.ds(pl.multiple_of(r0 * w, w), kh * w))
        deltas.append(r - r0)
    for p in range(q_ref.shape[-1] // LANES):
        cols = slice(p * LANES, (p + 1) * LANES)
        q = _scaled(q_ref[:, cols], head_dim ** -0.5)
        ck = ck_ref[:, cols]
        cv = cv_ref[:, cols]
        kw = [k_ref[win, cols] for win in windows]
        vw = [v_ref[win, cols] for win in windows]
        out = None
        for hh, msk in enumerate(masks):
            qm = jnp.where(msk, q, zero)
            s_cx = _nt_dot(qm, ck)
            s_nb = []
            for rr in range(NBR_ROWS):
                first = NA_KH - 1 - deltas[rr]
                bias = jnp.concatenate([tile_ref[2 * p + hh, first + 2 * jj] for jj in range(kh // 2)], axis=-1)
                s_nb.append(_nt_dot(qm[rr * w:(rr + 1) * w, :], kw[rr]) + bias)
            s_nb = jnp.concatenate(s_nb, axis=0)
            mx = jnp.maximum(jnp.max(s_nb, axis=-1, keepdims=True), jnp.max(s_cx, axis=-1, keepdims=True))
            e_nb = jnp.exp(s_nb - mx)
            e_cx = jnp.exp(s_cx - mx)
            den = jnp.sum(e_nb, axis=-1, keepdims=True) + jnp.sum(e_cx, axis=-1, keepdims=True)
            e_nb = e_nb.astype(BF16)
            o_nb = jnp.concatenate(
                [jnp.dot(e_nb[rr * w:(rr + 1) * w, :], jnp.where(msk, vw[rr], zero), preferred_element_type=F32)
                 for rr in range(NBR_ROWS)], axis=0)
            o = jnp.dot(e_cx.astype(BF16), jnp.where(msk, cv, zero), preferred_element_type=F32)
            o = (o + o_nb) / den
            out = o if out is None else out + o
        o_ref[:, cols] = out.astype(o_ref.dtype)


def neighbourhood_attention(q, k, v, ctx_k, ctx_v, tiles, layer, seq_len, head_dim):
    m, width = q.shape
    n_seq = m // seq_len
    rows = seq_len // GRID_W
    kh = min(NA_KH, rows)
    assert kh % 2 == 0 and rows % NBR_ROWS == 0
    groups = rows // NBR_ROWS
    past = ctx_k.shape[2]
    return pl.pallas_call(
        functools.partial(_nbr_attn_kernel, head_dim=head_dim, rows=rows, kh=kh),
        out_shape=jax.ShapeDtypeStruct((m, width), BF16),
        grid=(n_seq, groups),
        in_specs=[
            pl.BlockSpec((NBR_ROWS * GRID_W, width), lambda b, g: (b * groups + g, 0)),
            pl.BlockSpec((seq_len, width), lambda b, g: (b, 0)),
            pl.BlockSpec((seq_len, width), lambda b, g: (b, 0)),
            pl.BlockSpec((None, None, past, width), lambda b, g: (b, layer, 0, 0)),
            pl.BlockSpec((None, None, past, width), lambda b, g: (b, layer, 0, 0)),
            pl.BlockSpec((None,) + tiles.shape[1:], lambda b, g: (layer, 0, 0, 0, 0)),
        ],
        out_specs=pl.BlockSpec((NBR_ROWS * GRID_W, width), lambda b, g: (b * groups + g, 0)),
        compiler_params=_params("parallel", "parallel"),
        name="nbr_attn",
    )(q, k, v, ctx_k, ctx_v, tiles)


def _out_proj_kernel(y0_ref, y1_ref, y2_ref, y3_ref, w_ref, x_ref, mod_ref, g_ref, o_ref, *, d, gw):
    gg = g_ref[...] * mod_ref[:, 2 * d:3 * d]
    for r in range(x_ref.shape[0] // ROW_CHUNK):
        rows = slice(r * ROW_CHUNK, (r + 1) * ROW_CHUNK)
        mix = None
        for gi, y_ref in enumerate((y0_ref, y1_ref, y2_ref, y3_ref)):
            part = jnp.dot(y_ref[rows, :], w_ref[gi * gw:(gi + 1) * gw, :], preferred_element_type=F32)
            mix = part if mix is None else mix + part
        rs = lax.rsqrt(jnp.mean(mix * mix, axis=-1, keepdims=True) + EPS)
        o_ref[rows, :] = x_ref[rows, :] + mix * rs * gg


def out_proj(ys, w_out, x, mods, g_post, layer, seq_len, cond_base, cond_stride, tm=512):
    m, d = x.shape
    gw = ys[0].shape[-1]
    tm = min(tm, m)
    cidx = functools.partial(_cond_index, tm=tm, seq_len=seq_len, cond_base=cond_base, cond_stride=cond_stride)
    yspec = pl.BlockSpec((tm, gw), lambda i: (i, 0))
    return pl.pallas_call(
        functools.partial(_out_proj_kernel, d=d, gw=gw),
        out_shape=jax.ShapeDtypeStruct((m, d), F32),
        grid=(m // tm,),
        in_specs=[
            yspec, yspec, yspec, yspec,
            pl.BlockSpec((None, w_out.shape[1], d), lambda i: (layer, 0, 0), pipeline_mode=pl.Buffered(1)),
            pl.BlockSpec((tm, d), lambda i: (i, 0)),
            pl.BlockSpec((None, None, 1, N_MOD * d), lambda i: (layer, cidx(i), 0, 0)),
            pl.BlockSpec((None, 1, d), lambda i: (layer, 0, 0)),
        ],
        out_specs=pl.BlockSpec((tm, d), lambda i: (i, 0)),
        compiler_params=_params("parallel"),
        name="out_proj",
    )(*ys, w_out, x, mods, g_post)


def _ffn_kernel(x_ref, mod_ref, gpre_ref, gpost_ref, w1_ref, w3_ref, w2_ref, o_ref, h_ref, *, d, rb):
    j = pl.program_id(1)

    @pl.when(j == 0)
    def _():
        _norm_modulate(x_ref, gpre_ref[...], mod_ref[:, 3 * d:4 * d], mod_ref[:, 4 * d:5 * d], h_ref)

    def step(first):
        w1, w3, w2 = w1_ref[...], w3_ref[...], w2_ref[...].astype(BF16)
        for r in range(h_ref.shape[0] // rb):
            rows = slice(r * rb, (r + 1) * rb)
            h = h_ref[rows, :]
            a = jnp.dot(h, w1, preferred_element_type=F32)
            b = jnp.dot(h, w3, preferred_element_type=F32)
            hid = (a * _sigmoid(a) * b).astype(BF16)
            part = jnp.dot(hid, w2, preferred_element_type=F32)
            if first:
                o_ref[rows, :] = part
            else:
                o_ref[rows, :] += part

    last = pl.num_programs(1) - 1

    @pl.when(j == 0)
    def _():
        step(True)

    @pl.when((j > 0) & (j < last))
    def _():
        step(False)

    @pl.when(j == last)
    def _():
        step(False)
        _gated_residual(x_ref, o_ref, gpost_ref[...], mod_ref[:, 5 * d:6 * d], o_ref)


def ffn(x, mods, g_pre, g_post, w1, w3, w2, layer, seq_len, cond_base, cond_stride, tm=1024, tf=512, rb=512):
    m, d = x.shape
    f = w1.shape[-1]
    tm = min(tm, m)
    rb = min(rb, tm)
    cidx = functools.partial(_cond_index, tm=tm, seq_len=seq_len, cond_base=cond_base, cond_stride=cond_stride)
    return pl.pallas_call(
        functools.partial(_ffn_kernel, d=d, rb=rb),
        out_shape=jax.ShapeDtypeStruct((m, d), F32),
        grid=(m // tm, f // tf),
        in_specs=[
            pl.BlockSpec((tm, d), lambda i, j: (i, 0)),
            pl.BlockSpec((None, None, 1, N_MOD * d), lambda i, j: (layer, cidx(i), 0, 0)),
            pl.BlockSpec((None, 1, d), lambda i, j: (layer, 0, 0)),
            pl.BlockSpec((None, 1, d), lambda i, j: (layer, 0, 0)),
            pl.BlockSpec((None, d, tf), lambda i, j: (layer, 0, j)),
            pl.BlockSpec((None, d, tf), lambda i, j: (layer, 0, j)),
            pl.BlockSpec((None, tf, d), lambda i, j: (layer, j, 0)),
        ],
        out_specs=pl.BlockSpec((tm, d), lambda i, j: (i, 0)),
        scratch_shapes=[pltpu.VMEM((tm, d), BF16)],
        compiler_params=_params("parallel", "arbitrary"),
        name="ffn",
    )(x, mods, g_pre, g_post, w1, w3, w2)


def _block_diag_gates(gate_w):
    depth, _, _, nblk, c, _ = gate_w.shape
    per = LANES // c
    w = gate_w.reshape(depth, 4, nblk // per, per, c, c).astype(BF16)
    eye = jnp.eye(per, dtype=BF16)
    tiles = jnp.einsum('lgnpce,pq->lgnpcqe', w, eye)
    return tiles.reshape(depth, 4, nblk // per, LANES, LANES)


def kernel(x_prompt, x_sample, cache_k, cache_v, state_lru, c, c_ctx, w_ada, b_ada, g_pre_mix, g_post_mix,
           g_pre_ffn, g_post_ffn, w_in, pool_w, pool_scale, lru_conv_w, lru_conv_b, lru_gate_w, lru_gate_b,
           lru_lambda, cm_dw_w, cm_dw_b, cm_ln_g, cm_ln_b, cm_pw_w, cm_pw_b, na_rpb, w_out, ffn_w1, ffn_w3, ffn_w2):
    batch, seq, d = x_prompt.shape
    dec_batch, dec_seq, _ = x_sample.shape
    depth = w_in.shape[0]
    gw = pool_scale.shape[-1]
    heads, head_dim = cache_k.shape[-2], cache_k.shape[-1]
    past = cache_k.shape[2]
    assert dec_batch + 1 <= 8 and seq % SEQ_CHUNK == 0 and dec_seq % SEQ_CHUNK == 0

    cond = jnp.zeros((8, d), F32).at[0].set(c_ctx).at[1:1 + dec_batch].set(c)
    mods = adaln_all(cond, w_ada, b_ada).reshape(depth, 8, 1, N_MOD * d)

    vec = lambda a: a.reshape(depth, 1, a.shape[-1])
    g_pre_mix, g_post_mix, g_pre_ffn, g_post_ffn = map(vec, (g_pre_mix, g_post_mix, g_pre_ffn, g_post_ffn))
    pool_scale, lru_conv_b, cm_dw_b, cm_ln_g, cm_ln_b, cm_pw_b = map(
        vec, (pool_scale, lru_conv_b, cm_dw_b, cm_ln_g, cm_ln_b, cm_pw_b))
    w_in_b, w_out_b = w_in.astype(BF16), w_out.astype(BF16)
    w1_b, w3_b = ffn_w1.astype(BF16), ffn_w3.astype(BF16)
    pool_w_b, cm_pw_b16 = pool_w.astype(BF16), cm_pw_w.astype(BF16)
    gate_tiles = _block_diag_gates(lru_gate_w)
    gate_b = lru_gate_b.reshape(depth, 4, gw)
    tiles = rpb_tiles(na_rpb)
    ctx_k = cache_k.reshape(dec_batch, depth, past, gw).astype(BF16)
    ctx_v = cache_v.reshape(dec_batch, depth, past, gw).astype(BF16)

    def layer_step(x, l, seq_len, cond_base, cond_stride, h0, kv):
        res = in_proj(x, mods, g_pre_mix, w_in_b, l, seq_len, cond_base, cond_stride, kv=kv)
        u_pool, u_lru_x, u_lru_g, u_cm_a, u_cm_g, u_q, u_k, u_v = res[:8]
        y_pool, y_cm = pool_conformer(u_pool, u_cm_a, u_cm_g, pool_w_b, pool_scale, cm_dw_w, cm_dw_b, cm_ln_g, cm_ln_b,
                                      cm_pw_b16, cm_pw_b, l, seq_len)
        y_lru, st = lru_mixer(u_lru_x, u_lru_g, lru_conv_w, lru_conv_b, gate_tiles, gate_b, lru_lambda, h0, l, seq_len)
        if kv is None:
            y_att = neighbourhood_attention(u_q, u_k, u_v, ctx_k, ctx_v, tiles, l, seq_len, head_dim)
        else:
            y_att = context_attention(u_q, u_k, u_v, seq_len, head_dim)
        x = out_proj((y_pool, y_lru, y_cm, y_att), w_out_b, x, mods, g_post_mix, l, seq_len, cond_base, cond_stride)
        x = ffn(x, mods, g_pre_ffn, g_post_ffn, w1_b, w3_b, ffn_w2, l, seq_len, cond_base, cond_stride)
        return x, st, tuple(res[8:])

    xp = x_prompt.reshape(batch * seq, d)
    xs = x_sample.reshape(dec_batch * dec_seq, d)
    zeros_h0 = jnp.zeros((batch, 2, gw), F32)
    kv = tuple(jnp.zeros((batch, depth, seq, gw), F32) for _ in range(2))
    sts = []
    for l in range(depth):
        xp, st, kv = layer_step(xp, l, seq, 0, 0, zeros_h0, kv)
        sts.append(st)
        xs, _, _ = layer_step(xs, l, dec_seq, 1, 1, state_lru[:, l], None)

    out_kv = (batch, depth, seq, heads, head_dim)
    return (xp.reshape(batch, seq, d), xs.reshape(dec_batch, dec_seq, d),
            kv[0].reshape(out_kv), kv[1].reshape(out_kv), jnp.stack(sts, axis=1))
```
